```python
import jax, jax.numpy as jnp
from jax import lax
import numpy as np

D_MODEL = 1024
BATCH = 4
SEQ = 4096
DEPTH = 2

RET_HEADS = 4
RET_QK_DIM = D_MODEL // RET_HEADS
RET_V_DIM = 2 * RET_QK_DIM
RET_QK_W = RET_HEADS * RET_QK_DIM
RET_V_W = RET_HEADS * RET_V_DIM
RET_CHUNK = 128
ROPE_BASE = 10000.0
SB_HEADS = 16
SB_DIM = 64
SB_W = SB_HEADS * SB_DIM
SB_BLOCK = 128
D_FF = 4 * D_MODEL
N_MOD = 6
EPS = 1e-6
GN_EPS = 1e-5
IN_SPLITS = (RET_QK_W, RET_QK_W, RET_V_W, RET_V_W, SB_W, SB_W, SB_W, D_MODEL, D_MODEL)
IN_W = sum(IN_SPLITS)

kernel_name = "hybrid_retention_stickbreaking_gated_block"


def _rmsnorm(x, g):
    xf = x.astype(jnp.float32)
    y = xf * lax.rsqrt(jnp.mean(xf * xf, axis=-1, keepdims=True) + EPS)
    return (y * g.astype(jnp.float32)).astype(x.dtype)


def _modulate(h, shift, scale):
    return h * (1.0 + scale[:, None, :]) + shift[:, None, :]


def _heads(t, n_heads):
    b, s, w = t.shape
    return t.reshape(b, s, n_heads, w // n_heads).transpose(0, 2, 1, 3)


def _merge_heads(t):
    b, h, s, d = t.shape
    return t.transpose(0, 2, 1, 3).reshape(b, s, h * d)


def _rotary(t):
    s, d = t.shape[2], t.shape[3]
    half = d // 2
    inv_freq = jnp.power(ROPE_BASE, -jnp.arange(half, dtype=jnp.float32) / half)
    ang = jnp.arange(s, dtype=jnp.float32)[:, None] * inv_freq[None, :]
    cos, sin = jnp.cos(ang), jnp.sin(ang)
    t1, t2 = t[..., :half], t[..., half:]
    return jnp.concatenate([t1 * cos - t2 * sin, t1 * sin + t2 * cos], axis=-1)


def _retention(q, k, v):
    out_dtype = v.dtype
    q = q.astype(jnp.float32)
    k = k.astype(jnp.float32) * (q.shape[-1] ** -0.5)
    v = v.astype(jnp.float32)
    b, h, s, dk = q.shape
    dv = v.shape[-1]
    c = RET_CHUNK
    n = s // c
    log_gamma = jnp.log1p(-jnp.power(2.0, -5.0 - jnp.arange(h, dtype=jnp.float32)))
    idx = jnp.arange(c, dtype=jnp.float32)
    rel = idx[:, None] - idx[None, :]
    decay = jnp.where(rel >= 0, jnp.exp(jnp.maximum(rel, 0.0) * log_gamma[:, None, None]), 0.0)
    xi = jnp.exp((idx + 1.0) * log_gamma[:, None])
    zeta = jnp.exp((c - 1.0 - idx) * log_gamma[:, None])
    gamma_c = jnp.exp(c * log_gamma)

    qc = q.reshape(b, h, n, c, dk)
    kc = k.reshape(b, h, n, c, dk)
    vc = v.reshape(b, h, n, c, dv)
    scores = jnp.einsum('bhncd,bhnmd->bhncm', qc, kc) * decay[None, :, None]
    o_inner = jnp.einsum('bhncm,bhnme->bhnce', scores, vc)

    def step(r, qkv):
        qn, kn, vn = qkv
        cross = jnp.einsum('bhcd,bhde->bhce', qn, r) * xi[None, :, :, None]
        r_new = r * gamma_c[None, :, None, None] + jnp.einsum(
            'bhcd,bhce->bhde', kn * zeta[None, :, :, None], vn)
        return r_new, cross

    r0 = jnp.zeros((b, h, dk, dv), jnp.float32)
    xs = (jnp.moveaxis(qc, 2, 0), jnp.moveaxis(kc, 2, 0), jnp.moveaxis(vc, 2, 0))
    _, o_cross = lax.scan(step, r0, xs)
    o = (o_inner + jnp.moveaxis(o_cross, 0, 2)).reshape(b, h, s, dv)
    mu = jnp.mean(o, axis=-1, keepdims=True)
    var = jnp.mean(jnp.square(o - mu), axis=-1, keepdims=True)
    return ((o - mu) * lax.rsqrt(var + GN_EPS)).astype(out_dtype)


def _stick_breaking(q, k, v):
    out_dtype = v.dtype
    b, h, s, d = q.shape
    nb = s // SB_BLOCK
    kf = k.astype(jnp.float32)
    vf = v.astype(jnp.float32)
    qb = jnp.moveaxis(q.astype(jnp.float32).reshape(b, h, nb, SB_BLOCK, d), 2, 0)
    kpos = jnp.arange(s)
    scale = d ** -0.5

    def block(args):
        qi, i = args
        qpos = i * SB_BLOCK + jnp.arange(SB_BLOCK)
        causal = kpos[None, :] < qpos[:, None]
        z = jnp.einsum('bhtd,bhsd->bhts', qi, kf) * scale
        log_1m_beta = jnp.where(causal, -jax.nn.softplus(z), 0.0)
        rev_incl = lax.cumsum(log_1m_beta, axis=3, reverse=True)
        rev_excl = jnp.concatenate([rev_incl[..., 1:], jnp.zeros_like(rev_incl[..., :1])], axis=-1)
        a = jnp.where(causal, jnp.exp(jax.nn.log_sigmoid(z) + rev_excl), 0.0)
        return jnp.einsum('bhts,bhsd->bhtd', a, vf)

    out = lax.map(block, (qb, jnp.arange(nb)))
    return jnp.moveaxis(out, 0, 2).reshape(b, h, s, d).astype(out_dtype)


def _mixer(h, w_in, w_ret_out, w_sb_out, w_mix_out):
    p = h @ w_in
    offsets = [int(o) for o in np.cumsum(IN_SPLITS)[:-1]]
    rq, rk, rv, rg, sq, sk, sv, ga, gb = jnp.split(p, offsets, axis=-1)
    y_ret = _retention(_rotary(_heads(rq, RET_HEADS)), _rotary(_heads(rk, RET_HEADS)), _heads(rv, RET_HEADS))
    y_a = (jax.nn.silu(rg) * _merge_heads(y_ret)) @ w_ret_out
    y_sb = _stick_breaking(_heads(sq, SB_HEADS), _heads(sk, SB_HEADS), _heads(sv, SB_HEADS))
    y_b = _merge_heads(y_sb) @ w_sb_out
    merged = jax.nn.sigmoid(ga) * y_a + jax.nn.sigmoid(gb) * y_b
    return merged @ w_mix_out


def setup_inputs(seed: int = 0) -> dict:
    key = jax.random.key(seed)
    ks = jax.random.split(key, 16)
    f32 = jnp.float32

    def w(k, shape, fan_in):
        return jax.random.normal(k, shape, f32) * (fan_in ** -0.5)

    def gain(k, shape):
        return 1.0 + 0.02 * jax.random.normal(k, shape, f32)

    return {
        "x": jax.random.normal(ks[0], (BATCH, SEQ, D_MODEL), f32),
        "c": jax.random.normal(ks[1], (BATCH, D_MODEL), f32),
        "norm_mix_g": gain(ks[2], (DEPTH, D_MODEL)),
        "w_in": w(ks[3], (DEPTH, D_MODEL, IN_W), D_MODEL),
        "w_ret_out": w(ks[4], (DEPTH, RET_V_W, D_MODEL), RET_V_W),
        "w_sb_out": w(ks[5], (DEPTH, SB_W, D_MODEL), SB_W),
        "w_mix_out": w(ks[6], (DEPTH, D_MODEL, D_MODEL), D_MODEL),
        "norm_mlp_g": gain(ks[7], (DEPTH, D_MODEL)),
        "w_up": w(ks[8], (DEPTH, D_MODEL, D_FF), D_MODEL),
        "w_down": w(ks[9], (DEPTH, D_FF, D_MODEL), D_FF),
        "w_ada": w(ks[10], (DEPTH, D_MODEL, N_MOD * D_MODEL), D_MODEL),
        "b_ada": 0.02 * jax.random.normal(ks[11], (DEPTH, N_MOD * D_MODEL), f32),
        "final_g": gain(ks[12], (D_MODEL,)),
    }


def reference(x, c, norm_mix_g, w_in, w_ret_out, w_sb_out, w_mix_out, norm_mlp_g,
              w_up, w_down, w_ada, b_ada, final_g):
    c_act = jax.nn.silu(c)
    for l in range(DEPTH):
        mod = c_act @ w_ada[l] + b_ada[l]
        shift1, scale1, gate1, shift2, scale2, gate2 = jnp.split(mod, N_MOD, axis=-1)
        h = _modulate(_rmsnorm(x, norm_mix_g[l]), shift1, scale1)
        x = x + gate1[:, None, :] * _mixer(h, w_in[l], w_ret_out[l], w_sb_out[l], w_mix_out[l])
        h2 = _modulate(_rmsnorm(x, norm_mlp_g[l]), shift2, scale2)
        u = jnp.square(jax.nn.relu(h2 @ w_up[l]))
        x = x + gate2[:, None, :] * (u @ w_down[l])
    return _rmsnorm(x, final_g)
```

```python
import functools

import jax
import jax.numpy as jnp
from jax import lax
from jax.experimental import pallas as pl
from jax.experimental.pallas import tpu as pltpu

F32 = jnp.float32
BF16 = jnp.bfloat16

RET_HEADS = 4
SB_HEADS = 16
SB_DIM = 64
N_MOD = 6
ROPE_BASE = 10000.0
EPS = 1e-6
GN_EPS = 1e-5

RET_CHUNK = 256
SB_TILE = 256
LANES = 128
MIB = 1024 * 1024


def _params(semantics, vmem_mib):
    return pltpu.CompilerParams(dimension_semantics=semantics, vmem_limit_bytes=vmem_mib * MIB)


def _ada_kernel(c_ref, w_ref, b_ref, o_ref):
    c = c_ref[...]
    c_act = c * jax.nn.sigmoid(c)
    o_ref[0] = jnp.dot(c_act, w_ref[0], preferred_element_type=F32) + b_ref[0]


def _ada_modulation(c, w_ada, b_ada):
    depth, d, n = w_ada.shape
    b = c.shape[0]
    tn = n // 4
    return pl.pallas_call(
        _ada_kernel,
        grid=(depth, n // tn),
        in_specs=[
            pl.BlockSpec((b, d), lambda l, j: (0, 0)),
            pl.BlockSpec((1, d, tn), lambda l, j: (l, 0, j)),
            pl.BlockSpec((1, 1, tn), lambda l, j: (l, 0, j)),
        ],
        out_specs=pl.BlockSpec((1, b, tn), lambda l, j: (l, 0, j)),
        out_shape=jax.ShapeDtypeStruct((depth, b, n), F32),
        compiler_params=_params(("parallel", "parallel"), 32),
        name="ada_modulation",
    )(c, w_ada, b_ada.reshape(depth, 1, n))


def _norm_modulate(x, g, shift, scale):
    ms = jnp.mean(x * x, axis=-1, keepdims=True)
    y = x * lax.rsqrt(ms + EPS) * g
    return y * (1.0 + scale) + shift


def _inproj_kernel(x_ref, g_ref, mod_ref, w_ref, o_ref, h_ref):
    @pl.when(pl.program_id(2) == 0)
    def _():
        h = _norm_modulate(x_ref[0], g_ref[...], mod_ref[0, 0:1, :], mod_ref[0, 1:2, :])
        h_ref[...] = h.astype(BF16)

    o_ref[0] = jnp.dot(h_ref[...], w_ref[...], preferred_element_type=F32).astype(BF16)


def _in_projection(x, g, mod, w_in):
    b, s, d = x.shape
    n = w_in.shape[1]
    tm, tn = 512, n // 4
    return pl.pallas_call(
        _inproj_kernel,
        grid=(b, s // tm, n // tn),
        in_specs=[
            pl.BlockSpec((1, tm, d), lambda bi, i, j: (bi, i, 0)),
            pl.BlockSpec((1, d), lambda bi, i, j: (0, 0)),
            pl.BlockSpec((1, N_MOD, d), lambda bi, i, j: (bi, 0, 0)),
            pl.BlockSpec((d, tn), lambda bi, i, j: (0, j)),
        ],
        out_specs=pl.BlockSpec((1, tm, tn), lambda bi, i, j: (bi, i, j)),
        out_shape=jax.ShapeDtypeStruct((b, s, n), BF16),
        scratch_shapes=[pltpu.VMEM((tm, d), BF16)],
        compiler_params=_params(("parallel", "parallel", "arbitrary"), 48),
        name="in_projection",
    )(x, g.reshape(1, d), mod, w_in)


def _retention_kernel(lg_ref, q_ref, k_ref, v_ref, rg_ref, cos_ref, sin_ref, o_ref,
                      r_ref, dec_ref, xi_ref, zeta_ref):
    hd = pl.program_id(1)
    c = q_ref.shape[1]
    dk = q_ref.shape[2]
    half = dk // 2
    lg = lg_ref[0, hd]
    gamma_c = lg_ref[1, hd]

    @pl.when(pl.program_id(2) == 0)
    def _():
        r_ref[...] = jnp.zeros_like(r_ref)
        rel = (lax.broadcasted_iota(jnp.int32, (c, c), 0)
               - lax.broadcasted_iota(jnp.int32, (c, c), 1)).astype(F32)
        dec_ref[...] = jnp.where(rel >= 0, jnp.exp(jnp.maximum(rel, 0.0) * lg), 0.0)
        idx = lax.broadcasted_iota(jnp.int32, (c, 1), 0).astype(F32)
        xi_ref[...] = jnp.exp((idx + 1.0) * lg)
        zeta_ref[...] = jnp.exp((c - 1.0 - idx) * lg)

    cos = cos_ref[...]
    sin = sin_ref[...]

    def rot(t):
        t1, t2 = t[:, :half], t[:, half:]
        return jnp.concatenate([t1 * cos - t2 * sin, t1 * sin + t2 * cos], axis=-1)

    qr = rot(q_ref[0].astype(F32)).astype(BF16)
    kr = rot(k_ref[0].astype(F32)) * (dk ** -0.5)
    v = v_ref[0]
    scores = lax.dot_general(qr, kr.astype(BF16), (((1,), (1,)), ((), ())),
                             preferred_element_type=F32) * dec_ref[...]
    r_old = r_ref[...]
    o = jnp.dot(scores.astype(BF16), v, preferred_element_type=F32)
    o = o + jnp.dot(qr, r_old.astype(BF16), preferred_element_type=F32) * xi_ref[...]
    kz = (kr * zeta_ref[...]).astype(BF16)
    r_ref[...] = r_old * gamma_c + lax.dot_general(kz, v, (((0,), (0,)), ((), ())),
                                                   preferred_element_type=F32)
    mu = jnp.mean(o, axis=-1, keepdims=True)
    oc = o - mu
    var = jnp.mean(oc * oc, axis=-1, keepdims=True)
    yn = oc * lax.rsqrt(var + GN_EPS)
    rg = rg_ref[0].astype(F32)
    o_ref[0] = (rg * jax.nn.sigmoid(rg) * yn).astype(BF16)


def _retention(p, lg_tab, cos, sin, d_model):
    b, s, _ = p.shape
    dk = d_model // RET_HEADS
    dv = 2 * dk
    c = RET_CHUNK
    kq, kk = 0, RET_HEADS
    kv, kg = (2 * d_model) // dv, (2 * d_model) // dv + RET_HEADS
    return pl.pallas_call(
        _retention_kernel,
        grid=(b, RET_HEADS, s // c),
        in_specs=[
            pl.BlockSpec(memory_space=pltpu.SMEM),
            pl.BlockSpec((1, c, dk), lambda bi, h, n: (bi, n, kq + h)),
            pl.BlockSpec((1, c, dk), lambda bi, h, n: (bi, n, kk + h)),
            pl.BlockSpec((1, c, dv), lambda bi, h, n: (bi, n, kv + h)),
            pl.BlockSpec((1, c, dv), lambda bi, h, n: (bi, n, kg + h)),
            pl.BlockSpec((c, dk // 2), lambda bi, h, n: (n, 0)),
            pl.BlockSpec((c, dk // 2), lambda bi, h, n: (n, 0)),
        ],
        out_specs=pl.BlockSpec((1, c, dv), lambda bi, h, n: (bi, n, h)),
        out_shape=jax.ShapeDtypeStruct((b, s, RET_HEADS * dv), BF16),
        scratch_shapes=[
            pltpu.VMEM((dk, dv), F32),
            pltpu.VMEM((c, c), F32),
            pltpu.VMEM((c, 1), F32),
            pltpu.VMEM((c, 1), F32),
        ],
        compiler_params=_params(("parallel", "parallel", "arbitrary"), 32),
        name="retention",
    )(lg_tab, p, p, p, p, cos, sin)


def _sb_kernel(q_ref, k_ref, v_ref, u_ref, o_ref, acc_ref, carry_ref):
    t = q_ref.shape[1]
    i = pl.program_id(2)
    lane = lax.broadcasted_iota(jnp.int32, (1, LANES), 1)
    first = lane < SB_DIM
    q = q_ref[0] * (SB_DIM ** -0.5)
    zero = jnp.zeros_like(q)
    q_heads = (jnp.where(first, q, zero), jnp.where(first, zero, q))
    acc_ref[...] = jnp.zeros_like(acc_ref)
    carry_ref[...] = jnp.zeros_like(carry_ref)
    causal = (lax.broadcasted_iota(jnp.int32, (t, t), 1)
              < lax.broadcasted_iota(jnp.int32, (t, t), 0))

    def block(j, masked):
        start = pl.multiple_of(j * t, t)
        k = k_ref[0, pl.ds(start, t), :]
        v = v_ref[0, pl.ds(start, t), :]
        u = u_ref[...]
        for hd in range(2):
            z = lax.dot_general(q_heads[hd], k, (((1,), (1,)), ((), ())),
                                preferred_element_type=F32)
            sp = jnp.maximum(z, 0.0) + jnp.log(1.0 + jnp.exp(-jnp.abs(z)))
            lmb = -sp
            if masked:
                lmb = jnp.where(causal, lmb, 0.0)
            hi = lmb.astype(BF16)
            lo = (lmb - hi.astype(F32)).astype(BF16)
            rev = (jnp.dot(hi, u, preferred_element_type=F32)
                   + jnp.dot(lo, u, preferred_element_type=F32))
            carry = carry_ref[hd]
            a = jnp.exp((z - sp) + rev + carry)
            if masked:
                a = jnp.where(causal, a, 0.0)
            acc_ref[hd] += jnp.dot(a.astype(BF16), v, preferred_element_type=F32)
            carry_ref[hd] = carry + jnp.sum(lmb, axis=-1, keepdims=True)

    block(i, True)

    def body(jj, _):
        block(i - 1 - jj, False)
        return 0

    lax.fori_loop(0, i, body, 0)
    o_ref[0] = jnp.where(first, acc_ref[0], acc_ref[1]).astype(BF16)


def _stick_breaking(p, u, d_model):
    b, s, _ = p.shape
    t = SB_TILE
    pairs = SB_HEADS * SB_DIM // LANES
    base = 6 * d_model // LANES
    return pl.pallas_call(
        _sb_kernel,
        grid=(b, pairs, s // t),
        in_specs=[
            pl.BlockSpec((1, t, LANES), lambda bi, hp, i: (bi, i, base + hp)),
            pl.BlockSpec((1, s, LANES), lambda bi, hp, i: (bi, 0, base + pairs + hp)),
            pl.BlockSpec((1, s, LANES), lambda bi, hp, i: (bi, 0, base + 2 * pairs + hp)),
            pl.BlockSpec((t, t), lambda bi, hp, i: (0, 0)),
        ],
        out_specs=pl.BlockSpec((1, t, LANES), lambda bi, hp, i: (bi, i, hp)),
        out_shape=jax.ShapeDtypeStruct((b, s, SB_HEADS * SB_DIM), BF16),
        scratch_shapes=[
            pltpu.VMEM((2, t, LANES), F32),
            pltpu.VMEM((2, t, 1), F32),
        ],
        compiler_params=_params(("parallel", "parallel", "parallel"), 32),
        name="stick_breaking",
    )(p, p, p, u)


def _mixout_kernel(yr_ref, ys_ref, ga_ref, gb_ref, x_ref, mod_ref, wr_ref, ws_ref, wm_ref, o_ref):
    ya = jnp.dot(yr_ref[0], wr_ref[...], preferred_element_type=F32)
    yb = jnp.dot(ys_ref[0], ws_ref[...], preferred_element_type=F32)
    merged = (jax.nn.sigmoid(ga_ref[0].astype(F32)) * ya
              + jax.nn.sigmoid(gb_ref[0].astype(F32)) * yb)
    out = jnp.dot(merged.astype(BF16), wm_ref[...], preferred_element_type=F32)
    o_ref[0] = x_ref[0] + mod_ref[0, 2:3, :] * out


def _mix_out(y_ret, y_sb, p, x, mod, w_ret_out, w_sb_out, w_mix_out):
    b, s, d = x.shape
    tm = 512
    ga_blk = (p.shape[2] - 2 * d) // d
    const = lambda bi, i: (0, 0)
    return pl.pallas_call(
        _mixout_kernel,
        grid=(b, s // tm),
        in_specs=[
            pl.BlockSpec((1, tm, y_ret.shape[2]), lambda bi, i: (bi, i, 0)),
            pl.BlockSpec((1, tm, y_sb.shape[2]), lambda bi, i: (bi, i, 0)),
            pl.BlockSpec((1, tm, d), lambda bi, i: (bi, i, ga_blk)),
            pl.BlockSpec((1, tm, d), lambda bi, i: (bi, i, ga_blk + 1)),
            pl.BlockSpec((1, tm, d), lambda bi, i: (bi, i, 0)),
            pl.BlockSpec((1, N_MOD, d), lambda bi, i: (bi, 0, 0)),
            pl.BlockSpec(w_ret_out.shape, const),
            pl.BlockSpec(w_sb_out.shape, const),
            pl.BlockSpec(w_mix_out.shape, const),
        ],
        out_specs=pl.BlockSpec((1, tm, d), lambda bi, i: (bi, i, 0)),
        out_shape=jax.ShapeDtypeStruct((b, s, d), F32),
        compiler_params=_params(("parallel", "parallel"), 48),
        name="mix_out",
    )(y_ret, y_sb, p, p, x, mod, w_ret_out, w_sb_out, w_mix_out)


def _mlp_kernel(x_ref, g_ref, mod_ref, wu_ref, wd_ref, fg_ref, o_ref, h_ref, acc_ref, *, final_norm):
    f = pl.program_id(2)

    @pl.when(f == 0)
    def _():
        h = _norm_modulate(x_ref[0], g_ref[...], mod_ref[0, 3:4, :], mod_ref[0, 4:5, :])
        h_ref[...] = h.astype(BF16)
        acc_ref[...] = jnp.zeros_like(acc_ref)

    up = jnp.dot(h_ref[...], wu_ref[...], preferred_element_type=F32)
    u = jnp.square(jnp.maximum(up, 0.0))
    acc_ref[...] += jnp.dot(u.astype(BF16), wd_ref[...], preferred_element_type=F32)

    @pl.when(f == pl.num_programs(2) - 1)
    def _():
        y = x_ref[0] + mod_ref[0, 5:6, :] * acc_ref[...]
        if final_norm:
            ms = jnp.mean(y * y, axis=-1, keepdims=True)
            y = y * lax.rsqrt(ms + EPS) * fg_ref[...]
        o_ref[0] = y


def _mlp(x, g, mod, w_up, w_down, final_g, final_norm):
    b, s, d = x.shape
    ff = w_up.shape[1]
    tm, tf = 1024, 1024
    return pl.pallas_call(
        functools.partial(_mlp_kernel, final_norm=final_norm),
        grid=(b, s // tm, ff // tf),
        in_specs=[
            pl.BlockSpec((1, tm, d), lambda bi, i, f: (bi, i, 0)),
            pl.BlockSpec((1, d), lambda bi, i, f: (0, 0)),
            pl.BlockSpec((1, N_MOD, d), lambda bi, i, f: (bi, 0, 0)),
            pl.BlockSpec((d, tf), lambda bi, i, f: (0, f)),
            pl.BlockSpec((tf, d), lambda bi, i, f: (f, 0)),
            pl.BlockSpec((1, d), lambda bi, i, f: (0, 0)),
        ],
        out_specs=pl.BlockSpec((1, tm, d), lambda bi, i, f: (bi, i, 0)),
        out_shape=jax.ShapeDtypeStruct((b, s, d), F32),
        scratch_shapes=[pltpu.VMEM((tm, d), BF16), pltpu.VMEM((tm, d), F32)],
        compiler_params=_params(("parallel", "parallel", "arbitrary"), 48),
        name="mlp",
    )(x, g.reshape(1, d), mod, w_up, w_down, final_g.reshape(1, d))


def _rotary_tables(s, dk):
    half = dk // 2
    inv_freq = jnp.power(ROPE_BASE, -jnp.arange(half, dtype=F32) / half)
    ang = jnp.arange(s, dtype=F32)[:, None] * inv_freq[None, :]
    return jnp.cos(ang), jnp.sin(ang)


def _decay_table():
    log_gamma = jnp.log1p(-jnp.power(2.0, -5.0 - jnp.arange(RET_HEADS, dtype=F32)))
    return jnp.stack([log_gamma, jnp.exp(RET_CHUNK * log_gamma)])


def kernel(x, c, norm_mix_g, w_in, w_ret_out, w_sb_out, w_mix_out, norm_mlp_g, w_up, w_down,
           w_ada, b_ada, final_g):
    depth = w_in.shape[0]
    b, s, d = x.shape
    cos, sin = _rotary_tables(s, d // RET_HEADS)
    lg_tab = _decay_table()
    t = SB_TILE
    u = (lax.broadcasted_iota(jnp.int32, (t, t), 0)
         > lax.broadcasted_iota(jnp.int32, (t, t), 1)).astype(BF16)

    mod_all = _ada_modulation(c, w_ada, b_ada).reshape(depth, b, N_MOD, d)
    for l in range(depth):
        mod = mod_all[l]
        p = _in_projection(x, norm_mix_g[l], mod, w_in[l].astype(BF16))
        y_ret = _retention(p, lg_tab, cos, sin, d)
        y_sb = _stick_breaking(p, u, d)
        x = _mix_out(y_ret, y_sb, p, x, mod, w_ret_out[l].astype(BF16),
                     w_sb_out[l].astype(BF16), w_mix_out[l].astype(BF16))
        x = _mlp(x, norm_mlp_g[l], mod, w_up[l].astype(BF16), w_down[l].astype(BF16),
                 final_g, final_norm=(l == depth - 1))
    return x
```

```python
import functools

import jax
import jax.numpy as jnp
from jax import lax
from jax.experimental import pallas as pl
from jax.experimental.pallas import tpu as pltpu

F32 = jnp.float32
BF16 = jnp.bfloat16

RET_HEADS = 4
SB_HEADS = 16
SB_DIM = 64
N_MOD = 6
ROPE_BASE = 10000.0
EPS = 1e-6
GN_EPS = 1e-5

RET_CHUNK = 256
SB_SUB = 256
SB_Q_SUBS = 2
NEG_BIG = -1e30
LOG2E = 1.4426950408889634
LANES = 128
MIB = 1024 * 1024


def _params(semantics, vmem_mib):
    return pltpu.CompilerParams(dimension_semantics=semantics, vmem_limit_bytes=vmem_mib * MIB)


def _ada_kernel(c_ref, w_ref, b_ref, o_ref):
    c = c_ref[...]
    c_act = c * jax.nn.sigmoid(c)
    o_ref[0] = jnp.dot(c_act, w_ref[0], preferred_element_type=F32) + b_ref[0]


def _ada_modulation(c, w_ada, b_ada):
    depth, d, n = w_ada.shape
    b = c.shape[0]
    tn = n // 4
    return pl.pallas_call(
        _ada_kernel,
        grid=(depth, n // tn),
        in_specs=[
            pl.BlockSpec((b, d), lambda l, j: (0, 0)),
            pl.BlockSpec((1, d, tn), lambda l, j: (l, 0, j)),
            pl.BlockSpec((1, 1, tn), lambda l, j: (l, 0, j)),
        ],
        out_specs=pl.BlockSpec((1, b, tn), lambda l, j: (l, 0, j)),
        out_shape=jax.ShapeDtypeStruct((depth, b, n), F32),
        compiler_params=_params(("parallel", "parallel"), 32),
        name="ada_modulation",
    )(c, w_ada, b_ada.reshape(depth, 1, n))


def _norm_modulate(x, g, shift, scale):
    ms = jnp.mean(x * x, axis=-1, keepdims=True)
    y = x * lax.rsqrt(ms + EPS) * g
    return y * (1.0 + scale) + shift


def _inproj_kernel(x_ref, g_ref, mod_ref, w_ref, o_ref, h_ref):
    @pl.when(pl.program_id(2) == 0)
    def _():
        h = _norm_modulate(x_ref[0], g_ref[...], mod_ref[0, 0:1, :], mod_ref[0, 1:2, :])
        h_ref[...] = h.astype(BF16)

    o_ref[0] = jnp.dot(h_ref[...], w_ref[...], preferred_element_type=F32).astype(BF16)


def _in_projection(x, g, mod, w_in):
    b, s, d = x.shape
    n = w_in.shape[1]
    tm, tn = 512, n // 4
    return pl.pallas_call(
        _inproj_kernel,
        grid=(b, s // tm, n // tn),
        in_specs=[
            pl.BlockSpec((1, tm, d), lambda bi, i, j: (bi, i, 0)),
            pl.BlockSpec((1, d), lambda bi, i, j: (0, 0)),
            pl.BlockSpec((1, N_MOD, d), lambda bi, i, j: (bi, 0, 0)),
            pl.BlockSpec((d, tn), lambda bi, i, j: (0, j)),
        ],
        out_specs=pl.BlockSpec((1, tm, tn), lambda bi, i, j: (bi, i, j)),
        out_shape=jax.ShapeDtypeStruct((b, s, n), BF16),
        scratch_shapes=[pltpu.VMEM((tm, d), BF16)],
        compiler_params=_params(("parallel", "parallel", "arbitrary"), 48),
        name="in_projection",
    )(x, g.reshape(1, d), mod, w_in)


def _retention_kernel(lg_ref, q_ref, k_ref, v_ref, rg_ref, cos_ref, sin_ref, o_ref,
                      r_ref, dec_ref, xi_ref, zeta_ref):
    hd = pl.program_id(1)
    c = q_ref.shape[1]
    dk = q_ref.shape[2]
    half = dk // 2
    lg = lg_ref[0, hd]
    gamma_c = lg_ref[1, hd]

    @pl.when(pl.program_id(2) == 0)
    def _():
        r_ref[...] = jnp.zeros_like(r_ref)
        rel = (lax.broadcasted_iota(jnp.int32, (c, c), 0)
               - lax.broadcasted_iota(jnp.int32, (c, c), 1)).astype(F32)
        dec_ref[...] = jnp.where(rel >= 0, jnp.exp(jnp.maximum(rel, 0.0) * lg), 0.0)
        idx = lax.broadcasted_iota(jnp.int32, (c, 1), 0).astype(F32)
        xi_ref[...] = jnp.exp((idx + 1.0) * lg)
        zeta_ref[...] = jnp.exp((c - 1.0 - idx) * lg)

    cos = cos_ref[...]
    sin = sin_ref[...]

    def rot(t):
        t1, t2 = t[:, :half], t[:, half:]
        return jnp.concatenate([t1 * cos - t2 * sin, t1 * sin + t2 * cos], axis=-1)

    qr = rot(q_ref[0].astype(F32)).astype(BF16)
    kr = rot(k_ref[0].astype(F32)) * (dk ** -0.5)
    v = v_ref[0]
    scores = lax.dot_general(qr, kr.astype(BF16), (((1,), (1,)), ((), ())),
                             preferred_element_type=F32) * dec_ref[...]
    r_old = r_ref[...]
    o = jnp.dot(scores.astype(BF16), v, preferred_element_type=F32)
    o = o + jnp.dot(qr, r_old.astype(BF16), preferred_element_type=F32) * xi_ref[...]
    kz = (kr * zeta_ref[...]).astype(BF16)
    r_ref[...] = r_old * gamma_c + lax.dot_general(kz, v, (((0,), (0,)), ((), ())),
                                                   preferred_element_type=F32)
    mu = jnp.mean(o, axis=-1, keepdims=True)
    oc = o - mu
    var = jnp.mean(oc * oc, axis=-1, keepdims=True)
    yn = oc * lax.rsqrt(var + GN_EPS)
    rg = rg_ref[0].astype(F32)
    o_ref[0] = (rg * jax.nn.sigmoid(rg) * yn).astype(BF16)


def _retention(p, lg_tab, cos, sin, d_model):
    b, s, _ = p.shape
    dk = d_model // RET_HEADS
    dv = 2 * dk
    c = RET_CHUNK
    kq, kk = 0, RET_HEADS
    kv, kg = (2 * d_model) // dv, (2 * d_model) // dv + RET_HEADS
    return pl.pallas_call(
        _retention_kernel,
        grid=(b, RET_HEADS, s // c),
        in_specs=[
            pl.BlockSpec(memory_space=pltpu.SMEM),
            pl.BlockSpec((1, c, dk), lambda bi, h, n: (bi, n, kq + h)),
            pl.BlockSpec((1, c, dk), lambda bi, h, n: (bi, n, kk + h)),
            pl.BlockSpec((1, c, dv), lambda bi, h, n: (bi, n, kv + h)),
            pl.BlockSpec((1, c, dv), lambda bi, h, n: (bi, n, kg + h)),
            pl.BlockSpec((c, dk // 2), lambda bi, h, n: (n, 0)),
            pl.BlockSpec((c, dk // 2), lambda bi, h, n: (n, 0)),
        ],
        out_specs=pl.BlockSpec((1, c, dv), lambda bi, h, n: (bi, n, h)),
        out_shape=jax.ShapeDtypeStruct((b, s, RET_HEADS * dv), BF16),
        scratch_shapes=[
            pltpu.VMEM((dk, dv), F32),
            pltpu.VMEM((c, c), F32),
            pltpu.VMEM((c, 1), F32),
            pltpu.VMEM((c, 1), F32),
        ],
        compiler_params=_params(("parallel", "parallel", "arbitrary"), 32),
        name="retention",
    )(lg_tab, p, p, p, p, cos, sin)


def _sb_kernel(q_ref, k_ref, v_ref, u_ref, o_ref, qs_ref, acc_ref, carry_ref, z0_ref, z1_ref):
    t = SB_SUB
    n_sub = SB_Q_SUBS
    rows = 2 * t
    i = pl.program_id(2)
    lane = lax.broadcasted_iota(jnp.int32, (1, LANES), 1)
    first = lane < SB_DIM
    for s in range(n_sub):
        q = q_ref[0, s * t:(s + 1) * t, :] * (SB_DIM ** -0.5)
        zero = jnp.zeros_like(q)
        qs_ref[s * rows:s * rows + t, :] = jnp.where(first, q, zero)
        qs_ref[s * rows + t:(s + 1) * rows, :] = jnp.where(first, zero, q)
    acc_ref[...] = jnp.zeros_like(acc_ref)
    carry_ref[...] = jnp.zeros_like(carry_ref)

    def scores(j, first_sub):
        k = k_ref[0, pl.ds(pl.multiple_of(j * t, t), t), :]
        return lax.dot_general(qs_ref[first_sub * rows:, :], k, (((1,), (1,)), ((), ())),
                               preferred_element_type=F32)

    def block(z_all, j, first_sub, diag_sub):
        r0 = first_sub * rows
        v = v_ref[0, pl.ds(pl.multiple_of(j * t, t), t), :]
        zs, lhs, sums = [], [], []
        for s in range(first_sub, n_sub):
            z = z_all[(s - first_sub) * rows:(s - first_sub + 1) * rows]
            if s == diag_sub:
                causal = (lax.broadcasted_iota(jnp.int32, (rows, t), 1)
                          < (lax.broadcasted_iota(jnp.int32, (rows, t), 0) & (t - 1)))
                z = jnp.where(causal, z, NEG_BIG)
            sp = jnp.maximum(z, 0.0) + jnp.log(1.0 + jnp.exp2(jnp.abs(z) * (-LOG2E)))
            hi = sp.astype(BF16)
            lo = (sp - hi.astype(F32)).astype(BF16)
            zs.append(z)
            lhs.append(jnp.concatenate([hi, lo], axis=1))
            sums.append(jnp.sum(sp, axis=-1, keepdims=True))
        s_incl = jnp.dot(jnp.concatenate(lhs, axis=0), u_ref[...], preferred_element_type=F32)
        carry = carry_ref[r0:, :]
        x = jnp.concatenate(zs, axis=0) - (s_incl + carry)
        a = jnp.exp2(x * LOG2E).astype(BF16)
        acc_ref[r0:, :] += jnp.dot(a, v, preferred_element_type=F32)
        carry_ref[r0:, :] = carry + jnp.concatenate(sums, axis=0)

    for s in reversed(range(n_sub)):
        block(scores(n_sub * i + s, s), n_sub * i + s, s, s)

    top = n_sub * i - 1
    z0_ref[...] = scores(jnp.maximum(top, 0), 0)

    def body(jj, _):
        j = top - 2 * jj
        z1_ref[...] = scores(j - 1, 0)
        block(z0_ref[...], j, 0, None)
        z0_ref[...] = scores(jnp.maximum(j - 2, 0), 0)
        block(z1_ref[...], j - 1, 0, None)
        return 0

    assert n_sub % 2 == 0
    lax.fori_loop(0, (n_sub // 2) * i, body, 0)
    for s in range(n_sub):
        o_ref[0, s * t:(s + 1) * t, :] = jnp.where(
            first, acc_ref[s * rows:s * rows + t, :], acc_ref[s * rows + t:(s + 1) * rows, :]).astype(BF16)


def _stick_breaking(p, u2, d_model):
    b, s, _ = p.shape
    tq = SB_Q_SUBS * SB_SUB
    pairs = SB_HEADS * SB_DIM // LANES
    base = 6 * d_model // LANES
    return pl.pallas_call(
        _sb_kernel,
        grid=(b, pairs, s // tq),
        in_specs=[
            pl.BlockSpec((1, tq, LANES), lambda bi, hp, i: (bi, i, base + hp)),
            pl.BlockSpec((1, s, LANES), lambda bi, hp, i: (bi, 0, base + pairs + hp)),
            pl.BlockSpec((1, s, LANES), lambda bi, hp, i: (bi, 0, base + 2 * pairs + hp)),
            pl.BlockSpec(u2.shape, lambda bi, hp, i: (0, 0)),
        ],
        out_specs=pl.BlockSpec((1, tq, LANES), lambda bi, hp, i: (bi, i, hp)),
        out_shape=jax.ShapeDtypeStruct((b, s, SB_HEADS * SB_DIM), BF16),
        scratch_shapes=[
            pltpu.VMEM((2 * tq, LANES), BF16),
            pltpu.VMEM((2 * tq, LANES), F32),
            pltpu.VMEM((2 * tq, 1), F32),
            pltpu.VMEM((2 * tq, SB_SUB), F32),
            pltpu.VMEM((2 * tq, SB_SUB), F32),
        ],
        compiler_params=_params(("parallel", "parallel", "parallel"), 32),
        name="stick_breaking",
    )(p, p, p, u2)


def _mixout_kernel(yr_ref, ys_ref, ga_ref, gb_ref, x_ref, mod_ref, wr_ref, ws_ref, wm_ref, o_ref):
    ya = jnp.dot(yr_ref[0], wr_ref[...], preferred_element_type=F32)
    yb = jnp.dot(ys_ref[0], ws_ref[...], preferred_element_type=F32)
    merged = (jax.nn.sigmoid(ga_ref[0].astype(F32)) * ya
              + jax.nn.sigmoid(gb_ref[0].astype(F32)) * yb)
    out = jnp.dot(merged.astype(BF16), wm_ref[...], preferred_element_type=F32)
    o_ref[0] = x_ref[0] + mod_ref[0, 2:3, :] * out


def _mix_out(y_ret, y_sb, p, x, mod, w_ret_out, w_sb_out, w_mix_out):
    b, s, d = x.shape
    tm = 512
    ga_blk = (p.shape[2] - 2 * d) // d
    const = lambda bi, i: (0, 0)
    return pl.pallas_call(
        _mixout_kernel,
        grid=(b, s // tm),
        in_specs=[
            pl.BlockSpec((1, tm, y_ret.shape[2]), lambda bi, i: (bi, i, 0)),
            pl.BlockSpec((1, tm, y_sb.shape[2]), lambda bi, i: (bi, i, 0)),
            pl.BlockSpec((1, tm, d), lambda bi, i: (bi, i, ga_blk)),
            pl.BlockSpec((1, tm, d), lambda bi, i: (bi, i, ga_blk + 1)),
            pl.BlockSpec((1, tm, d), lambda bi, i: (bi, i, 0)),
            pl.BlockSpec((1, N_MOD, d), lambda bi, i: (bi, 0, 0)),
            pl.BlockSpec(w_ret_out.shape, const),
            pl.BlockSpec(w_sb_out.shape, const),
            pl.BlockSpec(w_mix_out.shape, const),
        ],
        out_specs=pl.BlockSpec((1, tm, d), lambda bi, i: (bi, i, 0)),
        out_shape=jax.ShapeDtypeStruct((b, s, d), F32),
        compiler_params=_params(("parallel", "parallel"), 48),
        name="mix_out",
    )(y_ret, y_sb, p, p, x, mod, w_ret_out, w_sb_out, w_mix_out)


def _mlp_kernel(x_ref, g_ref, mod_ref, wu_ref, wd_ref, fg_ref, o_ref, h_ref, acc_ref, *, final_norm):
    f = pl.program_id(2)

    @pl.when(f == 0)
    def _():
        h = _norm_modulate(x_ref[0], g_ref[...], mod_ref[0, 3:4, :], mod_ref[0, 4:5, :])
        h_ref[...] = h.astype(BF16)
        acc_ref[...] = jnp.zeros_like(acc_ref)

    up = jnp.dot(h_ref[...], wu_ref[...], preferred_element_type=F32)
    u = jnp.square(jnp.maximum(up, 0.0))
    acc_ref[...] += jnp.dot(u.astype(BF16), wd_ref[...], preferred_element_type=F32)

    @pl.when(f == pl.num_programs(2) - 1)
    def _():
        y = x_ref[0] + mod_ref[0, 5:6, :] * acc_ref[...]
        if final_norm:
            ms = jnp.mean(y * y, axis=-1, keepdims=True)
            y = y * lax.rsqrt(ms + EPS) * fg_ref[...]
        o_ref[0] = y


def _mlp(x, g, mod, w_up, w_down, final_g, final_norm):
    b, s, d = x.shape
    ff = w_up.shape[1]
    tm, tf = 1024, 1024
    return pl.pallas_call(
        functools.partial(_mlp_kernel, final_norm=final_norm),
        grid=(b, s // tm, ff // tf),
        in_specs=[
            pl.BlockSpec((1, tm, d), lambda bi, i, f: (bi, i, 0)),
            pl.BlockSpec((1, d), lambda bi, i, f: (0, 0)),
            pl.BlockSpec((1, N_MOD, d), lambda bi, i, f: (bi, 0, 0)),
            pl.BlockSpec((d, tf), lambda bi, i, f: (0, f)),
            pl.BlockSpec((tf, d), lambda bi, i, f: (f, 0)),
            pl.BlockSpec((1, d), lambda bi, i, f: (0, 0)),
        ],
        out_specs=pl.BlockSpec((1, tm, d), lambda bi, i, f: (bi, i, 0)),
        out_shape=jax.ShapeDtypeStruct((b, s, d), F32),
        scratch_shapes=[pltpu.VMEM((tm, d), BF16), pltpu.VMEM((tm, d), F32)],
        compiler_params=_params(("parallel", "parallel", "arbitrary"), 48),
        name="mlp",
    )(x, g.reshape(1, d), mod, w_up, w_down, final_g.reshape(1, d))


def _rotary_tables(s, dk):
    half = dk // 2
    inv_freq = jnp.power(ROPE_BASE, -jnp.arange(half, dtype=F32) / half)
    ang = jnp.arange(s, dtype=F32)[:, None] * inv_freq[None, :]
    return jnp.cos(ang), jnp.sin(ang)


def _decay_table():
    log_gamma = jnp.log1p(-jnp.power(2.0, -5.0 - jnp.arange(RET_HEADS, dtype=F32)))
    return jnp.stack([log_gamma, jnp.exp(RET_CHUNK * log_gamma)])


def kernel(x, c, norm_mix_g, w_in, w_ret_out, w_sb_out, w_mix_out, norm_mlp_g, w_up, w_down,
           w_ada, b_ada, final_g):
    depth = w_in.shape[0]
    b, s, d = x.shape
    cos, sin = _rotary_tables(s, d // RET_HEADS)
    lg_tab = _decay_table()
    t = SB_SUB
    u = (lax.broadcasted_iota(jnp.int32, (t, t), 0)
         >= lax.broadcasted_iota(jnp.int32, (t, t), 1)).astype(BF16)
    u = jnp.concatenate([u, u], axis=0)

    mod_all = _ada_modulation(c, w_ada, b_ada).reshape(depth, b, N_MOD, d)
    for l in range(depth):
        mod = mod_all[l]
        p = _in_projection(x, norm_mix_g[l], mod, w_in[l].astype(BF16))
        y_ret = _retention(p, lg_tab, cos, sin, d)
        y_sb = _stick_breaking(p, u, d)
        x = _mix_out(y_ret, y_sb, p, x, mod, w_ret_out[l].astype(BF16),
                     w_sb_out[l].astype(BF16), w_mix_out[l].astype(BF16))
        x = _mlp(x, norm_mlp_g[l], mod, w_up[l].astype(BF16), w_down[l].astype(BF16),
                 final_g, final_norm=(l == depth - 1))
    return x
```

```python
import functools

import jax
import jax.numpy as jnp
from jax import lax
from jax.experimental import pallas as pl
from jax.experimental.pallas import tpu as pltpu

F32 = jnp.float32
BF16 = jnp.bfloat16

RET_HEADS = 4
SB_HEADS = 16
SB_DIM = 64
N_MOD = 6
ROPE_BASE = 10000.0
EPS = 1e-6
GN_EPS = 1e-5

RET_CHUNK = 256
SB_SUB = 256
SB_Q_SUBS = 4
NEG_BIG = -1e30
SB_SATURATED = 105.0
LOG2E = 1.4426950408889634
LANES = 128
MIB = 1024 * 1024


def _params(semantics, vmem_mib):
    return pltpu.CompilerParams(dimension_semantics=semantics, vmem_limit_bytes=vmem_mib * MIB)


def _ada_kernel(c_ref, w_ref, b_ref, o_ref):
    c = c_ref[...]
    c_act = c * jax.nn.sigmoid(c)
    o_ref[0] = jnp.dot(c_act, w_ref[0], preferred_element_type=F32) + b_ref[0]


def _ada_modulation(c, w_ada, b_ada):
    depth, d, n = w_ada.shape
    b = c.shape[0]
    tn = n // 4
    return pl.pallas_call(
        _ada_kernel,
        grid=(depth, n // tn),
        in_specs=[
            pl.BlockSpec((b, d), lambda l, j: (0, 0)),
            pl.BlockSpec((1, d, tn), lambda l, j: (l, 0, j)),
            pl.BlockSpec((1, 1, tn), lambda l, j: (l, 0, j)),
        ],
        out_specs=pl.BlockSpec((1, b, tn), lambda l, j: (l, 0, j)),
        out_shape=jax.ShapeDtypeStruct((depth, b, n), F32),
        compiler_params=_params(("parallel", "parallel"), 32),
        name="ada_modulation",
    )(c, w_ada, b_ada.reshape(depth, 1, n))


def _norm_modulate(x, g, shift, scale):
    ms = jnp.mean(x * x, axis=-1, keepdims=True)
    y = x * lax.rsqrt(ms + EPS) * g
    return y * (1.0 + scale) + shift


def _inproj_kernel(x_ref, g_ref, mod_ref, w_ref, o_ref, h_ref):
    @pl.when(pl.program_id(2) == 0)
    def _():
        h = _norm_modulate(x_ref[0], g_ref[...], mod_ref[0, 0:1, :], mod_ref[0, 1:2, :])
        h_ref[...] = h.astype(BF16)

    o_ref[0] = jnp.dot(h_ref[...], w_ref[...], preferred_element_type=F32).astype(BF16)


def _in_projection(x, g, mod, w_in):
    b, s, d = x.shape
    n = w_in.shape[1]
    tm, tn = 512, n // 4
    return pl.pallas_call(
        _inproj_kernel,
        grid=(b, s // tm, n // tn),
        in_specs=[
            pl.BlockSpec((1, tm, d), lambda bi, i, j: (bi, i, 0)),
            pl.BlockSpec((1, d), lambda bi, i, j: (0, 0)),
            pl.BlockSpec((1, N_MOD, d), lambda bi, i, j: (bi, 0, 0)),
            pl.BlockSpec((d, tn), lambda bi, i, j: (0, j)),
        ],
        out_specs=pl.BlockSpec((1, tm, tn), lambda bi, i, j: (bi, i, j)),
        out_shape=jax.ShapeDtypeStruct((b, s, n), BF16),
        scratch_shapes=[pltpu.VMEM((tm, d), BF16)],
        compiler_params=_params(("parallel", "parallel", "arbitrary"), 48),
        name="in_projection",
    )(x, g.reshape(1, d), mod, w_in)


def _retention_kernel(lg_ref, q_ref, k_ref, v_ref, rg_ref, cos_ref, sin_ref, o_ref,
                      r_ref, dec_ref, xi_ref, zeta_ref):
    hd = pl.program_id(1)
    c = q_ref.shape[1]
    dk = q_ref.shape[2]
    half = dk // 2
    lg = lg_ref[0, hd]
    gamma_c = lg_ref[1, hd]

    @pl.when(pl.program_id(2) == 0)
    def _():
        r_ref[...] = jnp.zeros_like(r_ref)
        rel = (lax.broadcasted_iota(jnp.int32, (c, c), 0)
               - lax.broadcasted_iota(jnp.int32, (c, c), 1)).astype(F32)
        dec_ref[...] = jnp.where(rel >= 0, jnp.exp(jnp.maximum(rel, 0.0) * lg), 0.0)
        idx = lax.broadcasted_iota(jnp.int32, (c, 1), 0).astype(F32)
        xi_ref[...] = jnp.exp((idx + 1.0) * lg)
        zeta_ref[...] = jnp.exp((c - 1.0 - idx) * lg)

    cos = cos_ref[...]
    sin = sin_ref[...]

    def rot(t):
        t1, t2 = t[:, :half], t[:, half:]
        return jnp.concatenate([t1 * cos - t2 * sin, t1 * sin + t2 * cos], axis=-1)

    qr = rot(q_ref[0].astype(F32)).astype(BF16)
    kr = rot(k_ref[0].astype(F32)) * (dk ** -0.5)
    v = v_ref[0]
    scores = lax.dot_general(qr, kr.astype(BF16), (((1,), (1,)), ((), ())),
                             preferred_element_type=F32) * dec_ref[...]
    r_old = r_ref[...]
    o = jnp.dot(scores.astype(BF16), v, preferred_element_type=F32)
    o = o + jnp.dot(qr, r_old.astype(BF16), preferred_element_type=F32) * xi_ref[...]
    kz = (kr * zeta_ref[...]).astype(BF16)
    r_ref[...] = r_old * gamma_c + lax.dot_general(kz, v, (((0,), (0,)), ((), ())),
                                                   preferred_element_type=F32)
    mu = jnp.mean(o, axis=-1, keepdims=True)
    oc = o - mu
    var = jnp.mean(oc * oc, axis=-1, keepdims=True)
    yn = oc * lax.rsqrt(var + GN_EPS)
    rg = rg_ref[0].astype(F32)
    o_ref[0] = (rg * jax.nn.sigmoid(rg) * yn).astype(BF16)


def _retention(p, lg_tab, cos, sin, d_model):
    b, s, _ = p.shape
    dk = d_model // RET_HEADS
    dv = 2 * dk
    c = RET_CHUNK
    kq, kk = 0, RET_HEADS
    kv, kg = (2 * d_model) // dv, (2 * d_model) // dv + RET_HEADS
    return pl.pallas_call(
        _retention_kernel,
        grid=(b, RET_HEADS, s // c),
        in_specs=[
            pl.BlockSpec(memory_space=pltpu.SMEM),
            pl.BlockSpec((1, c, dk), lambda bi, h, n: (bi, n, kq + h)),
            pl.BlockSpec((1, c, dk), lambda bi, h, n: (bi, n, kk + h)),
            pl.BlockSpec((1, c, dv), lambda bi, h, n: (bi, n, kv + h)),
            pl.BlockSpec((1, c, dv), lambda bi, h, n: (bi, n, kg + h)),
            pl.BlockSpec((c, dk // 2), lambda bi, h, n: (n, 0)),
            pl.BlockSpec((c, dk // 2), lambda bi, h, n: (n, 0)),
        ],
        out_specs=pl.BlockSpec((1, c, dv), lambda bi, h, n: (bi, n, h)),
        out_shape=jax.ShapeDtypeStruct((b, s, RET_HEADS * dv), BF16),
        scratch_shapes=[
            pltpu.VMEM((dk, dv), F32),
            pltpu.VMEM((c, c), F32),
            pltpu.VMEM((c, 1), F32),
            pltpu.VMEM((c, 1), F32),
        ],
        compiler_params=_params(("parallel", "parallel", "arbitrary"), 32),
        name="retention",
    )(lg_tab, p, p, p, p, cos, sin)


def _sb_kernel(q_ref, k_ref, v_ref, u_ref, o_ref, qs_ref, acc_ref, carry_ref):
    t = SB_SUB
    n_sub = SB_Q_SUBS
    rows = 2 * t
    i = pl.program_id(2)
    lane = lax.broadcasted_iota(jnp.int32, (1, LANES), 1)
    first = lane < SB_DIM
    for s in range(n_sub):
        q = q_ref[0, s * t:(s + 1) * t, :] * (SB_DIM ** -0.5)
        zero = jnp.zeros_like(q)
        qs_ref[s * rows:s * rows + t, :] = jnp.where(first, q, zero)
        qs_ref[s * rows + t:(s + 1) * rows, :] = jnp.where(first, zero, q)
    acc_ref[...] = jnp.zeros_like(acc_ref)
    carry_ref[...] = jnp.zeros_like(carry_ref)

    def sub(s):
        return slice(s * rows, (s + 1) * rows)

    def unit(s, j, diagonal=False, check_valid=False):
        start = pl.multiple_of(jnp.maximum(j, 0) * t, t)
        k = k_ref[0, pl.ds(start, t), :]
        v = v_ref[0, pl.ds(start, t), :]
        z = lax.dot_general(qs_ref[sub(s), :], k, (((1,), (1,)), ((), ())),
                            preferred_element_type=F32)
        if diagonal:
            causal = (lax.broadcasted_iota(jnp.int32, (rows, t), 1)
                      < (lax.broadcasted_iota(jnp.int32, (rows, t), 0) & (t - 1)))
            z = jnp.where(causal, z, NEG_BIG)
        if check_valid:
            z = jnp.where(j >= 0, z, NEG_BIG)
        sp = jnp.maximum(z, 0.0) + jnp.log(1.0 + jnp.exp2(jnp.abs(z) * (-LOG2E)))
        s_excl = jnp.dot(sp.astype(BF16), u_ref[...], preferred_element_type=F32)
        carry = carry_ref[sub(s), :]
        x = ((z - sp) - (s_excl + carry)) * LOG2E
        carry_ref[sub(s), :] = carry + jnp.sum(sp, axis=-1, keepdims=True)
        acc_ref[sub(s), :] += jnp.dot(jnp.exp2(x).astype(BF16), v, preferred_element_type=F32)

    base = n_sub * i
    for s in range(n_sub):
        unit(s, base + s, diagonal=True)
    for s in range(n_sub):
        unit(s, base + s - 1, check_valid=(s == 0))

    def more(state):
        w, min_carry = state
        return jnp.logical_and(w <= base + n_sub - 1, min_carry < SB_SATURATED)

    def wave(state):
        w, _ = state
        for s in range(n_sub):
            unit(s, base + s - w, check_valid=True)
        return w + 1, jnp.min(carry_ref[...])

    lax.while_loop(more, wave, (jnp.int32(2), jnp.min(carry_ref[...])))
    for s in range(n_sub):
        o_ref[0, s * t:(s + 1) * t, :] = jnp.where(
            first, acc_ref[s * rows:s * rows + t, :], acc_ref[s * rows + t:(s + 1) * rows, :]).astype(BF16)


def _stick_breaking(p, u2, d_model):
    b, s, _ = p.shape
    tq = SB_Q_SUBS * SB_SUB
    pairs = SB_HEADS * SB_DIM // LANES
    base = 6 * d_model // LANES
    return pl.pallas_call(
        _sb_kernel,
        grid=(b, pairs, s // tq),
        in_specs=[
            pl.BlockSpec((1, tq, LANES), lambda bi, hp, i: (bi, i, base + hp)),
            pl.BlockSpec((1, s, LANES), lambda bi, hp, i: (bi, 0, base + pairs + hp)),
            pl.BlockSpec((1, s, LANES), lambda bi, hp, i: (bi, 0, base + 2 * pairs + hp)),
            pl.BlockSpec(u2.shape, lambda bi, hp, i: (0, 0)),
        ],
        out_specs=pl.BlockSpec((1, tq, LANES), lambda bi, hp, i: (bi, i, hp)),
        out_shape=jax.ShapeDtypeStruct((b, s, SB_HEADS * SB_DIM), BF16),
        scratch_shapes=[
            pltpu.VMEM((2 * tq, LANES), BF16),
            pltpu.VMEM((2 * tq, LANES), F32),
            pltpu.VMEM((2 * tq, 1), F32),
        ],
        compiler_params=_params(("parallel", "parallel", "parallel"), 32),
        name="stick_breaking",
    )(p, p, p, u2)


def _mixout_kernel(yr_ref, ys_ref, ga_ref, gb_ref, x_ref, mod_ref, wr_ref, ws_ref, wm_ref, o_ref):
    ya = jnp.dot(yr_ref[0], wr_ref[...], preferred_element_type=F32)
    yb = jnp.dot(ys_ref[0], ws_ref[...], preferred_element_type=F32)
    merged = (jax.nn.sigmoid(ga_ref[0].astype(F32)) * ya
              + jax.nn.sigmoid(gb_ref[0].astype(F32)) * yb)
    out = jnp.dot(merged.astype(BF16), wm_ref[...], preferred_element_type=F32)
    o_ref[0] = x_ref[0] + mod_ref[0, 2:3, :] * out


def _mix_out(y_ret, y_sb, p, x, mod, w_ret_out, w_sb_out, w_mix_out):
    b, s, d = x.shape
    tm = 512
    ga_blk = (p.shape[2] - 2 * d) // d
    const = lambda bi, i: (0, 0)
    return pl.pallas_call(
        _mixout_kernel,
        grid=(b, s // tm),
        in_specs=[
            pl.BlockSpec((1, tm, y_ret.shape[2]), lambda bi, i: (bi, i, 0)),
            pl.BlockSpec((1, tm, y_sb.shape[2]), lambda bi, i: (bi, i, 0)),
            pl.BlockSpec((1, tm, d), lambda bi, i: (bi, i, ga_blk)),
            pl.BlockSpec((1, tm, d), lambda bi, i: (bi, i, ga_blk + 1)),
            pl.BlockSpec((1, tm, d), lambda bi, i: (bi, i, 0)),
            pl.BlockSpec((1, N_MOD, d), lambda bi, i: (bi, 0, 0)),
            pl.BlockSpec(w_ret_out.shape, const),
            pl.BlockSpec(w_sb_out.shape, const),
            pl.BlockSpec(w_mix_out.shape, const),
        ],
        out_specs=pl.BlockSpec((1, tm, d), lambda bi, i: (bi, i, 0)),
        out_shape=jax.ShapeDtypeStruct((b, s, d), F32),
        compiler_params=_params(("parallel", "parallel"), 48),
        name="mix_out",
    )(y_ret, y_sb, p, p, x, mod, w_ret_out, w_sb_out, w_mix_out)


def _mlp_kernel(x_ref, g_ref, mod_ref, wu_ref, wd_ref, fg_ref, o_ref, h_ref, acc_ref, *, final_norm):
    f = pl.program_id(2)

    @pl.when(f == 0)
    def _():
        h = _norm_modulate(x_ref[0], g_ref[...], mod_ref[0, 3:4, :], mod_ref[0, 4:5, :])
        h_ref[...] = h.astype(BF16)
        acc_ref[...] = jnp.zeros_like(acc_ref)

    up = jnp.dot(h_ref[...], wu_ref[...], preferred_element_type=F32)
    u = jnp.square(jnp.maximum(up, 0.0))
    acc_ref[...] += jnp.dot(u.astype(BF16), wd_ref[...], preferred_element_type=F32)

    @pl.when(f == pl.num_programs(2) - 1)
    def _():
        y = x_ref[0] + mod_ref[0, 5:6, :] * acc_ref[...]
        if final_norm:
            ms = jnp.mean(y * y, axis=-1, keepdims=True)
            y = y * lax.rsqrt(ms + EPS) * fg_ref[...]
        o_ref[0] = y


def _mlp(x, g, mod, w_up, w_down, final_g, final_norm):
    b, s, d = x.shape
    ff = w_up.shape[1]
    tm, tf = 1024, 1024
    return pl.pallas_call(
        functools.partial(_mlp_kernel, final_norm=final_norm),
        grid=(b, s // tm, ff // tf),
        in_specs=[
            pl.BlockSpec((1, tm, d), lambda bi, i, f: (bi, i, 0)),
            pl.BlockSpec((1, d), lambda bi, i, f: (0, 0)),
            pl.BlockSpec((1, N_MOD, d), lambda bi, i, f: (bi, 0, 0)),
            pl.BlockSpec((d, tf), lambda bi, i, f: (0, f)),
            pl.BlockSpec((tf, d), lambda bi, i, f: (f, 0)),
            pl.BlockSpec((1, d), lambda bi, i, f: (0, 0)),
        ],
        out_specs=pl.BlockSpec((1, tm, d), lambda bi, i, f: (bi, i, 0)),
        out_shape=jax.ShapeDtypeStruct((b, s, d), F32),
        scratch_shapes=[pltpu.VMEM((tm, d), BF16), pltpu.VMEM((tm, d), F32)],
        compiler_params=_params(("parallel", "parallel", "arbitrary"), 48),
        name="mlp",
    )(x, g.reshape(1, d), mod, w_up, w_down, final_g.reshape(1, d))


def _rotary_tables(s, dk):
    half = dk // 2
    inv_freq = jnp.power(ROPE_BASE, -jnp.arange(half, dtype=F32) / half)
    ang = jnp.arange(s, dtype=F32)[:, None] * inv_freq[None, :]
    return jnp.cos(ang), jnp.sin(ang)


def _decay_table():
    log_gamma = jnp.log1p(-jnp.power(2.0, -5.0 - jnp.arange(RET_HEADS, dtype=F32)))
    return jnp.stack([log_gamma, jnp.exp(RET_CHUNK * log_gamma)])


def kernel(x, c, norm_mix_g, w_in, w_ret_out, w_sb_out, w_mix_out, norm_mlp_g, w_up, w_down,
           w_ada, b_ada, final_g):
    depth = w_in.shape[0]
    b, s, d = x.shape
    cos, sin = _rotary_tables(s, d // RET_HEADS)
    lg_tab = _decay_table()
    t = SB_SUB
    u = (lax.broadcasted_iota(jnp.int32, (t, t), 0)
         > lax.broadcasted_iota(jnp.int32, (t, t), 1)).astype(BF16)

    mod_all = _ada_modulation(c, w_ada, b_ada).reshape(depth, b, N_MOD, d)
    for l in range(depth):
        mod = mod_all[l]
        p = _in_projection(x, norm_mix_g[l], mod, w_in[l].astype(BF16))
        y_ret = _retention(p, lg_tab, cos, sin, d)
        y_sb = _stick_breaking(p, u, d)
        x = _mix_out(y_ret, y_sb, p, x, mod, w_ret_out[l].astype(BF16),
                     w_sb_out[l].astype(BF16), w_mix_out[l].astype(BF16))
        x = _mlp(x, norm_mlp_g[l], mod, w_up[l].astype(BF16), w_down[l].astype(BF16),
                 final_g, final_norm=(l == depth - 1))
    return x
```

```python
import functools

import jax
import jax.numpy as jnp
from jax import lax
from jax.experimental import pallas as pl
from jax.experimental.pallas import tpu as pltpu

F32 = jnp.float32
BF16 = jnp.bfloat16

RET_HEADS = 4
SB_HEADS = 16
SB_DIM = 64
N_MOD = 6
ROPE_BASE = 10000.0
EPS = 1e-6
GN_EPS = 1e-5

RET_CHUNK = 256
SB_SUB = 256
SB_Q_SUBS = 4
NEG_BIG = -1e30
SB_SATURATED = 105.0
LOG2E = 1.4426950408889634
LANES = 128
MIB = 1024 * 1024


def _params(semantics, vmem_mib):
    return pltpu.CompilerParams(dimension_semantics=semantics, vmem_limit_bytes=vmem_mib * MIB)


def _ada_kernel(c_ref, w_ref, b_ref, o_ref):
    c = c_ref[...]
    c_act = c * jax.nn.sigmoid(c)
    o_ref[0] = jnp.dot(c_act, w_ref[0], preferred_element_type=F32) + b_ref[0]


def _ada_modulation(c, w_ada, b_ada):
    depth, d, n = w_ada.shape
    b = c.shape[0]
    tn = n // 4
    return pl.pallas_call(
        _ada_kernel,
        grid=(depth, n // tn),
        in_specs=[
            pl.BlockSpec((b, d), lambda l, j: (0, 0)),
            pl.BlockSpec((1, d, tn), lambda l, j: (l, 0, j)),
            pl.BlockSpec((1, 1, tn), lambda l, j: (l, 0, j)),
        ],
        out_specs=pl.BlockSpec((1, b, tn), lambda l, j: (l, 0, j)),
        out_shape=jax.ShapeDtypeStruct((depth, b, n), F32),
        compiler_params=_params(("parallel", "parallel"), 32),
        name="ada_modulation",
    )(c, w_ada, b_ada.reshape(depth, 1, n))


def _norm_modulate(x, g, shift, scale):
    ms = jnp.mean(x * x, axis=-1, keepdims=True)
    y = x * lax.rsqrt(ms + EPS) * g
    return y * (1.0 + scale) + shift


def _inproj_kernel(x_ref, g_ref, mod_ref, w_ref, o_ref, h_ref):
    @pl.when(pl.program_id(2) == 0)
    def _():
        h = _norm_modulate(x_ref[0], g_ref[...], mod_ref[0, 0:1, :], mod_ref[0, 1:2, :])
        h_ref[...] = h.astype(BF16)

    o_ref[0] = jnp.dot(h_ref[...], w_ref[...], preferred_element_type=F32).astype(BF16)


def _in_projection(x, g, mod, w_in):
    b, s, d = x.shape
    n = w_in.shape[1]
    tm, tn = 1024, n // 4
    return pl.pallas_call(
        _inproj_kernel,
        grid=(b, s // tm, n // tn),
        in_specs=[
            pl.BlockSpec((1, tm, d), lambda bi, i, j: (bi, i, 0)),
            pl.BlockSpec((1, d), lambda bi, i, j: (0, 0)),
            pl.BlockSpec((1, N_MOD, d), lambda bi, i, j: (bi, 0, 0)),
            pl.BlockSpec((d, tn), lambda bi, i, j: (0, j)),
        ],
        out_specs=pl.BlockSpec((1, tm, tn), lambda bi, i, j: (bi, i, j)),
        out_shape=jax.ShapeDtypeStruct((b, s, n), BF16),
        scratch_shapes=[pltpu.VMEM((tm, d), BF16)],
        compiler_params=_params(("parallel", "parallel", "arbitrary"), 56),
        name="in_projection",
    )(x, g.reshape(1, d), mod, w_in)


def _retention_kernel(lg_ref, q_ref, k_ref, v_ref, rg_ref, cos_ref, sin_ref, o_ref,
                      r_ref, dec_ref, xi_ref, zeta_ref):
    c = q_ref.shape[1]
    n_heads = r_ref.shape[0]
    dk, dv = r_ref.shape[1], r_ref.shape[2]
    half = dk // 2

    @pl.when(pl.program_id(1) == 0)
    def _():
        r_ref[...] = jnp.zeros_like(r_ref)
        rel = (lax.broadcasted_iota(jnp.int32, (c, c), 0)
               - lax.broadcasted_iota(jnp.int32, (c, c), 1)).astype(F32)
        idx = lax.broadcasted_iota(jnp.int32, (c, 1), 0).astype(F32)
        for hd in range(n_heads):
            lg = lg_ref[0, hd]
            dec_ref[hd] = jnp.where(rel >= 0, jnp.exp(jnp.maximum(rel, 0.0) * lg), 0.0)
            xi_ref[hd] = jnp.exp((idx + 1.0) * lg)
            zeta_ref[hd] = jnp.exp((c - 1.0 - idx) * lg)

    cos = cos_ref[...]
    sin = sin_ref[...]

    def rot(t):
        t1, t2 = t[:, :half], t[:, half:]
        return jnp.concatenate([t1 * cos - t2 * sin, t1 * sin + t2 * cos], axis=-1)

    for hd in range(n_heads):
        gamma_c = lg_ref[1, hd]
        qr = rot(q_ref[0, :, hd * dk:(hd + 1) * dk].astype(F32)).astype(BF16)
        kr = rot(k_ref[0, :, hd * dk:(hd + 1) * dk].astype(F32)) * (dk ** -0.5)
        v = v_ref[0, :, hd * dv:(hd + 1) * dv]
        scores = lax.dot_general(qr, kr.astype(BF16), (((1,), (1,)), ((), ())),
                                 preferred_element_type=F32) * dec_ref[hd]
        r_old = r_ref[hd]
        o = jnp.dot(scores.astype(BF16), v, preferred_element_type=F32)
        o = o + jnp.dot(qr, r_old.astype(BF16), preferred_element_type=F32) * xi_ref[hd]
        kz = (kr * zeta_ref[hd]).astype(BF16)
        r_ref[hd] = r_old * gamma_c + lax.dot_general(kz, v, (((0,), (0,)), ((), ())),
                                                      preferred_element_type=F32)
        mu = jnp.mean(o, axis=-1, keepdims=True)
        oc = o - mu
        var = jnp.mean(oc * oc, axis=-1, keepdims=True)
        yn = oc * lax.rsqrt(var + GN_EPS)
        rg = rg_ref[0, :, hd * dv:(hd + 1) * dv].astype(F32)
        o_ref[0, :, hd * dv:(hd + 1) * dv] = (rg * jax.nn.sigmoid(rg) * yn).astype(BF16)


def _retention(p, lg_tab, cos, sin, d_model):
    b, s, _ = p.shape
    dk = d_model // RET_HEADS
    dv = 2 * dk
    qk_w, v_w = RET_HEADS * dk, RET_HEADS * dv
    c = RET_CHUNK
    kq, kk, kv, kg = 0, 1, (2 * qk_w) // v_w, (2 * qk_w) // v_w + 1
    return pl.pallas_call(
        _retention_kernel,
        grid=(b, s // c),
        in_specs=[
            pl.BlockSpec(memory_space=pltpu.SMEM),
            pl.BlockSpec((1, c, qk_w), lambda bi, n: (bi, n, kq)),
            pl.BlockSpec((1, c, qk_w), lambda bi, n: (bi, n, kk)),
            pl.BlockSpec((1, c, v_w), lambda bi, n: (bi, n, kv)),
            pl.BlockSpec((1, c, v_w), lambda bi, n: (bi, n, kg)),
            pl.BlockSpec((c, dk // 2), lambda bi, n: (n, 0)),
            pl.BlockSpec((c, dk // 2), lambda bi, n: (n, 0)),
        ],
        out_specs=pl.BlockSpec((1, c, v_w), lambda bi, n: (bi, n, 0)),
        out_shape=jax.ShapeDtypeStruct((b, s, v_w), BF16),
        scratch_shapes=[
            pltpu.VMEM((RET_HEADS, dk, dv), F32),
            pltpu.VMEM((RET_HEADS, c, c), F32),
            pltpu.VMEM((RET_HEADS, c, 1), F32),
            pltpu.VMEM((RET_HEADS, c, 1), F32),
        ],
        compiler_params=_params(("parallel", "arbitrary"), 32),
        name="retention",
    )(lg_tab, p, p, p, p, cos, sin)


def _sb_kernel(q_ref, k_ref, v_ref, u_ref, o_ref, qs_ref, acc_ref, carry_ref):
    t = SB_SUB
    n_sub = SB_Q_SUBS
    rows = 2 * t
    i = pl.program_id(2)
    lane = lax.broadcasted_iota(jnp.int32, (1, LANES), 1)
    first = lane < SB_DIM
    for s in range(n_sub):
        q = q_ref[0, s * t:(s + 1) * t, :] * (SB_DIM ** -0.5)
        zero = jnp.zeros_like(q)
        qs_ref[s * rows:s * rows + t, :] = jnp.where(first, q, zero)
        qs_ref[s * rows + t:(s + 1) * rows, :] = jnp.where(first, zero, q)
    acc_ref[...] = jnp.zeros_like(acc_ref)
    carry_ref[...] = jnp.zeros_like(carry_ref)

    def sub(s):
        return slice(s * rows, (s + 1) * rows)

    def unit(s, j, diagonal=False, check_valid=False):
        start = pl.multiple_of(jnp.maximum(j, 0) * t, t)
        k = k_ref[0, pl.ds(start, t), :]
        v = v_ref[0, pl.ds(start, t), :]
        z = lax.dot_general(qs_ref[sub(s), :], k, (((1,), (1,)), ((), ())),
                            preferred_element_type=F32)
        if diagonal:
            causal = (lax.broadcasted_iota(jnp.int32, (rows, t), 1)
                      < (lax.broadcasted_iota(jnp.int32, (rows, t), 0) & (t - 1)))
            z = jnp.where(causal, z, NEG_BIG)
        if check_valid:
            z = jnp.where(j >= 0, z, NEG_BIG)
        sp = jnp.maximum(z, 0.0) + jnp.log(1.0 + jnp.exp2(jnp.abs(z) * (-LOG2E)))
        s_excl = jnp.dot(sp.astype(BF16), u_ref[...], preferred_element_type=F32)
        carry = carry_ref[sub(s), :]
        x = ((z - sp) - (s_excl + carry)) * LOG2E
        carry_ref[sub(s), :] = carry + jnp.sum(sp, axis=-1, keepdims=True)
        acc_ref[sub(s), :] += jnp.dot(jnp.exp2(x).astype(BF16), v, preferred_element_type=F32)

    base = n_sub * i
    for s in range(n_sub):
        unit(s, base + s, diagonal=True)
    for s in range(n_sub):
        unit(s, base + s - 1, check_valid=(s == 0))

    def wave_needed(w):
        need = jnp.bool_(False)
        for s in range(n_sub):
            unsaturated = jnp.min(carry_ref[sub(s), :]) < SB_SATURATED
            need = jnp.logical_or(need, jnp.logical_and(base + s - w >= 0, unsaturated))
        return need

    def wave(state):
        w, _ = state
        for s in range(n_sub):
            unit(s, base + s - w, check_valid=True)
        return w + 1, wave_needed(w + 1)

    lax.while_loop(lambda state: state[1], wave, (jnp.int32(2), wave_needed(2)))
    for s in range(n_sub):
        o_ref[0, s * t:(s + 1) * t, :] = jnp.where(
            first, acc_ref[s * rows:s * rows + t, :], acc_ref[s * rows + t:(s + 1) * rows, :]).astype(BF16)


def _stick_breaking(p, u2, d_model):
    b, s, _ = p.shape
    tq = SB_Q_SUBS * SB_SUB
    pairs = SB_HEADS * SB_DIM // LANES
    base = 6 * d_model // LANES
    return pl.pallas_call(
        _sb_kernel,
        grid=(b, pairs, s // tq),
        in_specs=[
            pl.BlockSpec((1, tq, LANES), lambda bi, hp, i: (bi, i, base + hp)),
            pl.BlockSpec((1, s, LANES), lambda bi, hp, i: (bi, 0, base + pairs + hp)),
            pl.BlockSpec((1, s, LANES), lambda bi, hp, i: (bi, 0, base + 2 * pairs + hp)),
            pl.BlockSpec(u2.shape, lambda bi, hp, i: (0, 0)),
        ],
        out_specs=pl.BlockSpec((1, tq, LANES), lambda bi, hp, i: (bi, i, hp)),
        out_shape=jax.ShapeDtypeStruct((b, s, SB_HEADS * SB_DIM), BF16),
        scratch_shapes=[
            pltpu.VMEM((2 * tq, LANES), BF16),
            pltpu.VMEM((2 * tq, LANES), F32),
            pltpu.VMEM((2 * tq, 1), F32),
        ],
        compiler_params=_params(("parallel", "parallel", "parallel"), 32),
        name="stick_breaking",
    )(p, p, p, u2)


def _mixout_kernel(yr_ref, ys_ref, ga_ref, gb_ref, x_ref, mod_ref, wr_ref, ws_ref, wm_ref, o_ref):
    ya = jnp.dot(yr_ref[0], wr_ref[...], preferred_element_type=F32)
    yb = jnp.dot(ys_ref[0], ws_ref[...], preferred_element_type=F32)
    merged = (jax.nn.sigmoid(ga_ref[0].astype(F32)) * ya
              + jax.nn.sigmoid(gb_ref[0].astype(F32)) * yb)
    out = jnp.dot(merged.astype(BF16), wm_ref[...], preferred_element_type=F32)
    o_ref[0] = x_ref[0] + mod_ref[0, 2:3, :] * out


def _mix_out(y_ret, y_sb, p, x, mod, w_ret_out, w_sb_out, w_mix_out):
    b, s, d = x.shape
    tm = 512
    ga_blk = (p.shape[2] - 2 * d) // d
    const = lambda bi, i: (0, 0)
    return pl.pallas_call(
        _mixout_kernel,
        grid=(b, s // tm),
        in_specs=[
            pl.BlockSpec((1, tm, y_ret.shape[2]), lambda bi, i: (bi, i, 0)),
            pl.BlockSpec((1, tm, y_sb.shape[2]), lambda bi, i: (bi, i, 0)),
            pl.BlockSpec((1, tm, d), lambda bi, i: (bi, i, ga_blk)),
            pl.BlockSpec((1, tm, d), lambda bi, i: (bi, i, ga_blk + 1)),
            pl.BlockSpec((1, tm, d), lambda bi, i: (bi, i, 0)),
            pl.BlockSpec((1, N_MOD, d), lambda bi, i: (bi, 0, 0)),
            pl.BlockSpec(w_ret_out.shape, const),
            pl.BlockSpec(w_sb_out.shape, const),
            pl.BlockSpec(w_mix_out.shape, const),
        ],
        out_specs=pl.BlockSpec((1, tm, d), lambda bi, i: (bi, i, 0)),
        out_shape=jax.ShapeDtypeStruct((b, s, d), F32),
        compiler_params=_params(("parallel", "parallel"), 48),
        name="mix_out",
    )(y_ret, y_sb, p, p, x, mod, w_ret_out, w_sb_out, w_mix_out)


def _mlp_kernel(x_ref, g_ref, mod_ref, wu_ref, wd_ref, fg_ref, o_ref, h_ref, acc_ref, *, final_norm):
    f = pl.program_id(2)

    @pl.when(f == 0)
    def _():
        h = _norm_modulate(x_ref[0], g_ref[...], mod_ref[0, 3:4, :], mod_ref[0, 4:5, :])
        h_ref[...] = h.astype(BF16)
        acc_ref[...] = jnp.zeros_like(acc_ref)

    up = jnp.dot(h_ref[...], wu_ref[...], preferred_element_type=F32)
    u = jnp.square(jnp.maximum(up, 0.0))
    acc_ref[...] += jnp.dot(u.astype(BF16), wd_ref[...], preferred_element_type=F32)

    @pl.when(f == pl.num_programs(2) - 1)
    def _():
        y = x_ref[0] + mod_ref[0, 5:6, :] * acc_ref[...]
        if final_norm:
            ms = jnp.mean(y * y, axis=-1, keepdims=True)
            y = y * lax.rsqrt(ms + EPS) * fg_ref[...]
        o_ref[0] = y


def _mlp(x, g, mod, w_up, w_down, final_g, final_norm):
    b, s, d = x.shape
    ff = w_up.shape[1]
    tm, tf = 1024, 1024
    return pl.pallas_call(
        functools.partial(_mlp_kernel, final_norm=final_norm),
        grid=(b, s // tm, ff // tf),
        in_specs=[
            pl.BlockSpec((1, tm, d), lambda bi, i, f: (bi, i, 0)),
            pl.BlockSpec((1, d), lambda bi, i, f: (0, 0)),
            pl.BlockSpec((1, N_MOD, d), lambda bi, i, f: (bi, 0, 0)),
            pl.BlockSpec((d, tf), lambda bi, i, f: (0, f)),
            pl.BlockSpec((tf, d), lambda bi, i, f: (f, 0)),
            pl.BlockSpec((1, d), lambda bi, i, f: (0, 0)),
        ],
        out_specs=pl.BlockSpec((1, tm, d), lambda bi, i, f: (bi, i, 0)),
        out_shape=jax.ShapeDtypeStruct((b, s, d), F32),
        scratch_shapes=[pltpu.VMEM((tm, d), BF16), pltpu.VMEM((tm, d), F32)],
        compiler_params=_params(("parallel", "parallel", "arbitrary"), 48),
        name="mlp",
    )(x, g.reshape(1, d), mod, w_up, w_down, final_g.reshape(1, d))


def _rotary_tables(s, dk):
    half = dk // 2
    inv_freq = jnp.power(ROPE_BASE, -jnp.arange(half, dtype=F32) / half)
    ang = jnp.arange(s, dtype=F32)[:, None] * inv_freq[None, :]
    return jnp.cos(ang), jnp.sin(ang)


def _decay_table():
    log_gamma = jnp.log1p(-jnp.power(2.0, -5.0 - jnp.arange(RET_HEADS, dtype=F32)))
    return jnp.stack([log_gamma, jnp.exp(RET_CHUNK * log_gamma)])


def kernel(x, c, norm_mix_g, w_in, w_ret_out, w_sb_out, w_mix_out, norm_mlp_g, w_up, w_down,
           w_ada, b_ada, final_g):
    depth = w_in.shape[0]
    b, s, d = x.shape
    cos, sin = _rotary_tables(s, d // RET_HEADS)
    lg_tab = _decay_table()
    t = SB_SUB
    u = (lax.broadcasted_iota(jnp.int32, (t, t), 0)
         > lax.broadcasted_iota(jnp.int32, (t, t), 1)).astype(BF16)

    mod_all = _ada_modulation(c, w_ada, b_ada).reshape(depth, b, N_MOD, d)
    for l in range(depth):
        mod = mod_all[l]
        p = _in_projection(x, norm_mix_g[l], mod, w_in[l].astype(BF16))
        y_ret = _retention(p, lg_tab, cos, sin, d)
        y_sb = _stick_breaking(p, u, d)
        x = _mix_out(y_ret, y_sb, p, x, mod, w_ret_out[l].astype(BF16),
                     w_sb_out[l].astype(BF16), w_mix_out[l].astype(BF16))
        x = _mlp(x, norm_mlp_g[l], mod, w_up[l].astype(BF16), w_down[l].astype(BF16),
                 final_g, final_norm=(l == depth - 1))
    return x
```

```python
import functools

import jax
import jax.numpy as jnp
from jax import lax
from jax.experimental import pallas as pl
from jax.experimental.pallas import tpu as pltpu

F32 = jnp.float32
BF16 = jnp.bfloat16

RET_HEADS = 4
SB_HEADS = 16
SB_DIM = 64
N_MOD = 6
ROPE_BASE = 10000.0
EPS = 1e-6
GN_EPS = 1e-5

RET_CHUNK = 256
SB_SUB = 256
SB_Q_SUBS = 4
SB_STRAIGHT_WAVES = 3
NEG_BIG = -1e30
SB_SATURATED = 105.0
LOG2E = 1.4426950408889634
LANES = 128
MIB = 1024 * 1024


def _params(semantics, vmem_mib):
    return pltpu.CompilerParams(dimension_semantics=semantics, vmem_limit_bytes=vmem_mib * MIB)


def _ada_kernel(c_ref, w_ref, b_ref, o_ref):
    c = c_ref[...]
    c_act = c * jax.nn.sigmoid(c)
    o_ref[0] = jnp.dot(c_act, w_ref[0], preferred_element_type=F32) + b_ref[0]


def _ada_modulation(c, w_ada, b_ada):
    depth, d, n = w_ada.shape
    b = c.shape[0]
    tn = n // 4
    return pl.pallas_call(
        _ada_kernel,
        grid=(depth, n // tn),
        in_specs=[
            pl.BlockSpec((b, d), lambda l, j: (0, 0)),
            pl.BlockSpec((1, d, tn), lambda l, j: (l, 0, j)),
            pl.BlockSpec((1, 1, tn), lambda l, j: (l, 0, j)),
        ],
        out_specs=pl.BlockSpec((1, b, tn), lambda l, j: (l, 0, j)),
        out_shape=jax.ShapeDtypeStruct((depth, b, n), F32),
        compiler_params=_params(("parallel", "parallel"), 32),
        name="ada_modulation",
    )(c, w_ada, b_ada.reshape(depth, 1, n))


def _norm_modulate(x, g, shift, scale):
    ms = jnp.mean(x * x, axis=-1, keepdims=True)
    y = x * lax.rsqrt(ms + EPS) * g
    return y * (1.0 + scale) + shift


def _inproj_kernel(x_ref, g_ref, mod_ref, w_ref, o_ref, h_ref):
    @pl.when(pl.program_id(2) == 0)
    def _():
        h = _norm_modulate(x_ref[0], g_ref[...], mod_ref[0, 0:1, :], mod_ref[0, 1:2, :])
        h_ref[...] = h.astype(BF16)

    o_ref[0] = jnp.dot(h_ref[...], w_ref[...], preferred_element_type=F32).astype(BF16)


def _in_projection(x, g, mod, w_in):
    b, s, d = x.shape
    n = w_in.shape[1]
    tm, tn = 1024, n // 4
    return pl.pallas_call(
        _inproj_kernel,
        grid=(b, s // tm, n // tn),
        in_specs=[
            pl.BlockSpec((1, tm, d), lambda bi, i, j: (bi, i, 0)),
            pl.BlockSpec((1, d), lambda bi, i, j: (0, 0)),
            pl.BlockSpec((1, N_MOD, d), lambda bi, i, j: (bi, 0, 0)),
            pl.BlockSpec((d, tn), lambda bi, i, j: (0, j)),
        ],
        out_specs=pl.BlockSpec((1, tm, tn), lambda bi, i, j: (bi, i, j)),
        out_shape=jax.ShapeDtypeStruct((b, s, n), BF16),
        scratch_shapes=[pltpu.VMEM((tm, d), BF16)],
        compiler_params=_params(("parallel", "parallel", "arbitrary"), 56),
        name="in_projection",
    )(x, g.reshape(1, d), mod, w_in)


def _retention_kernel(lg_ref, q_ref, k_ref, v_ref, rg_ref, cos_ref, sin_ref, o_ref,
                      r_ref, dec_ref, xi_ref, zeta_ref):
    c = q_ref.shape[1]
    n_heads = r_ref.shape[0]
    dk, dv = r_ref.shape[1], r_ref.shape[2]
    half = dk // 2

    @pl.when(pl.program_id(1) == 0)
    def _():
        r_ref[...] = jnp.zeros_like(r_ref)
        rel = (lax.broadcasted_iota(jnp.int32, (c, c), 0)
               - lax.broadcasted_iota(jnp.int32, (c, c), 1)).astype(F32)
        idx = lax.broadcasted_iota(jnp.int32, (c, 1), 0).astype(F32)
        for hd in range(n_heads):
            lg = lg_ref[0, hd]
            dec_ref[hd] = jnp.where(rel >= 0, jnp.exp(jnp.maximum(rel, 0.0) * lg), 0.0)
            xi_ref[hd] = jnp.exp((idx + 1.0) * lg)
            zeta_ref[hd] = jnp.exp((c - 1.0 - idx) * lg)

    cos = cos_ref[...]
    sin = sin_ref[...]

    def rot(t):
        t1, t2 = t[:, :half], t[:, half:]
        return jnp.concatenate([t1 * cos - t2 * sin, t1 * sin + t2 * cos], axis=-1)

    for hd in range(n_heads):
        gamma_c = lg_ref[1, hd]
        qr = rot(q_ref[0, :, hd * dk:(hd + 1) * dk].astype(F32)).astype(BF16)
        kr = rot(k_ref[0, :, hd * dk:(hd + 1) * dk].astype(F32)) * (dk ** -0.5)
        v = v_ref[0, :, hd * dv:(hd + 1) * dv]
        scores = lax.dot_general(qr, kr.astype(BF16), (((1,), (1,)), ((), ())),
                                 preferred_element_type=F32) * dec_ref[hd]
        r_old = r_ref[hd]
        o = jnp.dot(scores.astype(BF16), v, preferred_element_type=F32)
        o = o + jnp.dot(qr, r_old.astype(BF16), preferred_element_type=F32) * xi_ref[hd]
        kz = (kr * zeta_ref[hd]).astype(BF16)
        r_ref[hd] = r_old * gamma_c + lax.dot_general(kz, v, (((0,), (0,)), ((), ())),
                                                      preferred_element_type=F32)
        mu = jnp.mean(o, axis=-1, keepdims=True)
        oc = o - mu
        var = jnp.mean(oc * oc, axis=-1, keepdims=True)
        yn = oc * lax.rsqrt(var + GN_EPS)
        rg = rg_ref[0, :, hd * dv:(hd + 1) * dv].astype(F32)
        o_ref[0, :, hd * dv:(hd + 1) * dv] = (rg * jax.nn.sigmoid(rg) * yn).astype(BF16)


def _retention(p, lg_tab, cos, sin, d_model):
    b, s, _ = p.shape
    dk = d_model // RET_HEADS
    dv = 2 * dk
    qk_w, v_w = RET_HEADS * dk, RET_HEADS * dv
    c = RET_CHUNK
    kq, kk, kv, kg = 0, 1, (2 * qk_w) // v_w, (2 * qk_w) // v_w + 1
    return pl.pallas_call(
        _retention_kernel,
        grid=(b, s // c),
        in_specs=[
            pl.BlockSpec(memory_space=pltpu.SMEM),
            pl.BlockSpec((1, c, qk_w), lambda bi, n: (bi, n, kq)),
            pl.BlockSpec((1, c, qk_w), lambda bi, n: (bi, n, kk)),
            pl.BlockSpec((1, c, v_w), lambda bi, n: (bi, n, kv)),
            pl.BlockSpec((1, c, v_w), lambda bi, n: (bi, n, kg)),
            pl.BlockSpec((c, dk // 2), lambda bi, n: (n, 0)),
            pl.BlockSpec((c, dk // 2), lambda bi, n: (n, 0)),
        ],
        out_specs=pl.BlockSpec((1, c, v_w), lambda bi, n: (bi, n, 0)),
        out_shape=jax.ShapeDtypeStruct((b, s, v_w), BF16),
        scratch_shapes=[
            pltpu.VMEM((RET_HEADS, dk, dv), F32),
            pltpu.VMEM((RET_HEADS, c, c), F32),
            pltpu.VMEM((RET_HEADS, c, 1), F32),
            pltpu.VMEM((RET_HEADS, c, 1), F32),
        ],
        compiler_params=_params(("parallel", "arbitrary"), 32),
        name="retention",
    )(lg_tab, p, p, p, p, cos, sin)


def _sb_kernel(q_ref, k_ref, v_ref, u_ref, o_ref, qs_ref, acc_ref, carry_ref):
    t = SB_SUB
    n_sub = SB_Q_SUBS
    rows = 2 * t
    i = pl.program_id(2)
    lane = lax.broadcasted_iota(jnp.int32, (1, LANES), 1)
    first = lane < SB_DIM
    for s in range(n_sub):
        q = q_ref[0, s * t:(s + 1) * t, :] * (SB_DIM ** -0.5)
        zero = jnp.zeros_like(q)
        qs_ref[s * rows:s * rows + t, :] = jnp.where(first, q, zero)
        qs_ref[s * rows + t:(s + 1) * rows, :] = jnp.where(first, zero, q)
    acc_ref[...] = jnp.zeros_like(acc_ref)
    carry_ref[...] = jnp.zeros_like(carry_ref)

    def sub(s):
        return slice(s * rows, (s + 1) * rows)

    def unit(s, j, diagonal=False, check_valid=False):
        start = pl.multiple_of(jnp.maximum(j, 0) * t, t)
        k = k_ref[0, pl.ds(start, t), :]
        v = v_ref[0, pl.ds(start, t), :]
        z = lax.dot_general(qs_ref[sub(s), :], k, (((1,), (1,)), ((), ())),
                            preferred_element_type=F32)
        if diagonal:
            causal = (lax.broadcasted_iota(jnp.int32, (rows, t), 1)
                      < (lax.broadcasted_iota(jnp.int32, (rows, t), 0) & (t - 1)))
            z = jnp.where(causal, z, NEG_BIG)
        if check_valid:
            z = jnp.where(j >= 0, z, NEG_BIG)
        sp = jnp.maximum(z, 0.0) + jnp.log(1.0 + jnp.exp2(jnp.abs(z) * (-LOG2E)))
        s_excl = jnp.dot(sp.astype(BF16), u_ref[...], preferred_element_type=F32)
        carry = carry_ref[sub(s), :]
        x = ((z - sp) - (s_excl + carry)) * LOG2E
        carry_ref[sub(s), :] = carry + jnp.sum(sp, axis=-1, keepdims=True)
        acc_ref[sub(s), :] += jnp.dot(jnp.exp2(x).astype(BF16), v, preferred_element_type=F32)

    base = n_sub * i
    for w in range(SB_STRAIGHT_WAVES):
        for s in range(n_sub):
            unit(s, base + s - w, diagonal=(w == 0), check_valid=(s < w))

    def wave_needed(w):
        need = jnp.bool_(False)
        for s in range(n_sub):
            unsaturated = jnp.min(carry_ref[sub(s), :]) < SB_SATURATED
            need = jnp.logical_or(need, jnp.logical_and(base + s - w >= 0, unsaturated))
        return need

    def wave(state):
        w, _ = state
        for s in range(n_sub):
            unit(s, base + s - w, check_valid=True)
        return w + 1, wave_needed(w + 1)

    lax.while_loop(lambda state: state[1], wave,
                   (jnp.int32(SB_STRAIGHT_WAVES), wave_needed(SB_STRAIGHT_WAVES)))
    for s in range(n_sub):
        o_ref[0, s * t:(s + 1) * t, :] = jnp.where(
            first, acc_ref[s * rows:s * rows + t, :], acc_ref[s * rows + t:(s + 1) * rows, :]).astype(BF16)


def _stick_breaking(p, u2, d_model):
    b, s, _ = p.shape
    tq = SB_Q_SUBS * SB_SUB
    pairs = SB_HEADS * SB_DIM // LANES
    base = 6 * d_model // LANES
    return pl.pallas_call(
        _sb_kernel,
        grid=(b, pairs, s // tq),
        in_specs=[
            pl.BlockSpec((1, tq, LANES), lambda bi, hp, i: (bi, i, base + hp)),
            pl.BlockSpec((1, s, LANES), lambda bi, hp, i: (bi, 0, base + pairs + hp)),
            pl.BlockSpec((1, s, LANES), lambda bi, hp, i: (bi, 0, base + 2 * pairs + hp)),
            pl.BlockSpec(u2.shape, lambda bi, hp, i: (0, 0)),
        ],
        out_specs=pl.BlockSpec((1, tq, LANES), lambda bi, hp, i: (bi, i, hp)),
        out_shape=jax.ShapeDtypeStruct((b, s, SB_HEADS * SB_DIM), BF16),
        scratch_shapes=[
            pltpu.VMEM((2 * tq, LANES), BF16),
            pltpu.VMEM((2 * tq, LANES), F32),
            pltpu.VMEM((2 * tq, 1), F32),
        ],
        compiler_params=_params(("parallel", "parallel", "parallel"), 32),
        name="stick_breaking",
    )(p, p, p, u2)


def _mixout_kernel(yr_ref, ys_ref, ga_ref, gb_ref, x_ref, mod_ref, wr_ref, ws_ref, wm_ref, o_ref):
    ya = jnp.dot(yr_ref[0], wr_ref[...], preferred_element_type=F32)
    yb = jnp.dot(ys_ref[0], ws_ref[...], preferred_element_type=F32)
    merged = (jax.nn.sigmoid(ga_ref[0].astype(F32)) * ya
              + jax.nn.sigmoid(gb_ref[0].astype(F32)) * yb)
    out = jnp.dot(merged.astype(BF16), wm_ref[...], preferred_element_type=F32)
    o_ref[0] = x_ref[0] + mod_ref[0, 2:3, :] * out


def _mix_out(y_ret, y_sb, p, x, mod, w_ret_out, w_sb_out, w_mix_out):
    b, s, d = x.shape
    tm = 512
    ga_blk = (p.shape[2] - 2 * d) // d
    const = lambda bi, i: (0, 0)
    return pl.pallas_call(
        _mixout_kernel,
        grid=(b, s // tm),
        in_specs=[
            pl.BlockSpec((1, tm, y_ret.shape[2]), lambda bi, i: (bi, i, 0)),
            pl.BlockSpec((1, tm, y_sb.shape[2]), lambda bi, i: (bi, i, 0)),
            pl.BlockSpec((1, tm, d), lambda bi, i: (bi, i, ga_blk)),
            pl.BlockSpec((1, tm, d), lambda bi, i: (bi, i, ga_blk + 1)),
            pl.BlockSpec((1, tm, d), lambda bi, i: (bi, i, 0)),
            pl.BlockSpec((1, N_MOD, d), lambda bi, i: (bi, 0, 0)),
            pl.BlockSpec(w_ret_out.shape, const),
            pl.BlockSpec(w_sb_out.shape, const),
            pl.BlockSpec(w_mix_out.shape, const),
        ],
        out_specs=pl.BlockSpec((1, tm, d), lambda bi, i: (bi, i, 0)),
        out_shape=jax.ShapeDtypeStruct((b, s, d), F32),
        compiler_params=_params(("parallel", "parallel"), 48),
        name="mix_out",
    )(y_ret, y_sb, p, p, x, mod, w_ret_out, w_sb_out, w_mix_out)


def _mlp_kernel(x_ref, g_ref, mod_ref, wu_ref, wd_ref, fg_ref, o_ref, acc_ref, *, final_norm, tf):
    x = x_ref[0]
    h = _norm_modulate(x, g_ref[...], mod_ref[0, 3:4, :], mod_ref[0, 4:5, :]).astype(BF16)
    for f in range(wu_ref.shape[1] // tf):
        up = jnp.dot(h, wu_ref[:, f * tf:(f + 1) * tf], preferred_element_type=F32)
        u = jnp.square(jnp.maximum(up, 0.0)).astype(BF16)
        part = jnp.dot(u, wd_ref[f * tf:(f + 1) * tf, :], preferred_element_type=F32)
        if f == 0:
            acc_ref[...] = part
        else:
            acc_ref[...] += part
    y = x + mod_ref[0, 5:6, :] * acc_ref[...]
    if final_norm:
        ms = jnp.mean(y * y, axis=-1, keepdims=True)
        y = y * lax.rsqrt(ms + EPS) * fg_ref[...]
    o_ref[0] = y


def _mlp(x, g, mod, w_up, w_down, final_g, final_norm):
    b, s, d = x.shape
    ff = w_up.shape[1]
    tm, tf = 1024, 1024
    const = lambda bi, i: (0, 0)
    return pl.pallas_call(
        functools.partial(_mlp_kernel, final_norm=final_norm, tf=tf),
        grid=(b, s // tm),
        in_specs=[
            pl.BlockSpec((1, tm, d), lambda bi, i: (bi, i, 0)),
            pl.BlockSpec((1, d), const),
            pl.BlockSpec((1, N_MOD, d), lambda bi, i: (bi, 0, 0)),
            pl.BlockSpec((d, ff), const, pipeline_mode=pl.Buffered(1)),
            pl.BlockSpec((ff, d), const, pipeline_mode=pl.Buffered(1)),
            pl.BlockSpec((1, d), const),
        ],
        out_specs=pl.BlockSpec((1, tm, d), lambda bi, i: (bi, i, 0)),
        out_shape=jax.ShapeDtypeStruct((b, s, d), F32),
        scratch_shapes=[pltpu.VMEM((tm, d), F32)],
        compiler_params=_params(("parallel", "parallel"), 56),
        name="mlp",
    )(x, g.reshape(1, d), mod, w_up, w_down, final_g.reshape(1, d))


def _rotary_tables(s, dk):
    half = dk // 2
    inv_freq = jnp.power(ROPE_BASE, -jnp.arange(half, dtype=F32) / half)
    ang = jnp.arange(s, dtype=F32)[:, None] * inv_freq[None, :]
    return jnp.cos(ang), jnp.sin(ang)


def _decay_table():
    log_gamma = jnp.log1p(-jnp.power(2.0, -5.0 - jnp.arange(RET_HEADS, dtype=F32)))
    return jnp.stack([log_gamma, jnp.exp(RET_CHUNK * log_gamma)])


def kernel(x, c, norm_mix_g, w_in, w_ret_out, w_sb_out, w_mix_out, norm_mlp_g, w_up, w_down,
           w_ada, b_ada, final_g):
    depth = w_in.shape[0]
    b, s, d = x.shape
    cos, sin = _rotary_tables(s, d // RET_HEADS)
    lg_tab = _decay_table()
    t = SB_SUB
    u = (lax.broadcasted_iota(jnp.int32, (t, t), 0)
         > lax.broadcasted_iota(jnp.int32, (t, t), 1)).astype(BF16)

    mod_all = _ada_modulation(c, w_ada, b_ada).reshape(depth, b, N_MOD, d)
    for l in range(depth):
        mod = mod_all[l]
        p = _in_projection(x, norm_mix_g[l], mod, w_in[l].astype(BF16))
        y_ret = _retention(p, lg_tab, cos, sin, d)
        y_sb = _stick_breaking(p, u, d)
        x = _mix_out(y_ret, y_sb, p, x, mod, w_ret_out[l].astype(BF16),
                     w_sb_out[l].astype(BF16), w_mix_out[l].astype(BF16))
        x = _mlp(x, norm_mlp_g[l], mod, w_up[l].astype(BF16), w_down[l].astype(BF16),
                 final_g, final_norm=(l == depth - 1))
    return x
```

```python
import functools

import jax
import jax.numpy as jnp
from jax import lax
from jax.experimental import pallas as pl
from jax.experimental.pallas import tpu as pltpu

F32 = jnp.float32
BF16 = jnp.bfloat16

RET_HEADS = 4
SB_HEADS = 16
SB_DIM = 64
N_MOD = 6
ROPE_BASE = 10000.0
EPS = 1e-6
GN_EPS = 1e-5

RET_CHUNK = 256
SB_SUB = 256
SB_Q_SUBS = 8
SB_EXIT_GROUP = 4
SB_STRAIGHT_WAVES = 3
NEG_BIG = -1e30
SB_SATURATED = 105.0
LOG2E = 1.4426950408889634
LANES = 128
MIB = 1024 * 1024

ADA_COL_TILES = 4
IN_PROJ_ROWS = 1024
IN_PROJ_COL_TILES = 4
MIX_OUT_ROWS = 512
MLP_ROWS = 1024
MLP_FF_CHUNK = 1024
VMEM_LIMIT_MIB = {"ada": 32, "in_proj": 56, "retention": 32, "stick_breaking": 32,
                  "mix_out": 48, "mlp": 56}


def _params(semantics, call):
    return pltpu.CompilerParams(dimension_semantics=semantics,
                                vmem_limit_bytes=VMEM_LIMIT_MIB[call] * MIB)


def _ada_kernel(c_ref, w_ref, b_ref, o_ref):
    c = c_ref[...]
    c_act = c * jax.nn.sigmoid(c)
    o_ref[0] = jnp.dot(c_act, w_ref[0], preferred_element_type=F32) + b_ref[0]


def _ada_modulation(c, w_ada, b_ada):
    depth, d, n = w_ada.shape
    b = c.shape[0]
    tn = n // ADA_COL_TILES
    return pl.pallas_call(
        _ada_kernel,
        grid=(depth, n // tn),
        in_specs=[
            pl.BlockSpec((b, d), lambda l, j: (0, 0)),
            pl.BlockSpec((1, d, tn), lambda l, j: (l, 0, j)),
            pl.BlockSpec((1, 1, tn), lambda l, j: (l, 0, j)),
        ],
        out_specs=pl.BlockSpec((1, b, tn), lambda l, j: (l, 0, j)),
        out_shape=jax.ShapeDtypeStruct((depth, b, n), F32),
        compiler_params=_params(("parallel", "parallel"), "ada"),
        name="ada_modulation",
    )(c, w_ada, b_ada.reshape(depth, 1, n))


def _norm_modulate(x, g, shift, scale):
    ms = jnp.mean(x * x, axis=-1, keepdims=True)
    y = x * lax.rsqrt(ms + EPS) * g
    return y * (1.0 + scale) + shift


def _inproj_kernel(x_ref, g_ref, mod_ref, w_ref, o_ref, h_ref):
    @pl.when(pl.program_id(2) == 0)
    def _():
        h = _norm_modulate(x_ref[0], g_ref[...], mod_ref[0, 0:1, :], mod_ref[0, 1:2, :])
        h_ref[...] = h.astype(BF16)

    o_ref[0] = jnp.dot(h_ref[...], w_ref[...], preferred_element_type=F32).astype(BF16)


def _in_projection(x, g, mod, w_in):
    b, s, d = x.shape
    n = w_in.shape[1]
    tm, tn = IN_PROJ_ROWS, n // IN_PROJ_COL_TILES
    return pl.pallas_call(
        _inproj_kernel,
        grid=(b, s // tm, n // tn),
        in_specs=[
            pl.BlockSpec((1, tm, d), lambda bi, i, j: (bi, i, 0)),
            pl.BlockSpec((1, d), lambda bi, i, j: (0, 0)),
            pl.BlockSpec((1, N_MOD, d), lambda bi, i, j: (bi, 0, 0)),
            pl.BlockSpec((d, tn), lambda bi, i, j: (0, j)),
        ],
        out_specs=pl.BlockSpec((1, tm, tn), lambda bi, i, j: (bi, i, j)),
        out_shape=jax.ShapeDtypeStruct((b, s, n), BF16),
        scratch_shapes=[pltpu.VMEM((tm, d), BF16)],
        compiler_params=_params(("parallel", "parallel", "arbitrary"), "in_proj"),
        name="in_projection",
    )(x, g.reshape(1, d), mod, w_in)


def _retention_kernel(lg_ref, q_ref, k_ref, v_ref, rg_ref, cos_ref, sin_ref, o_ref,
                      r_ref, dec_ref, xi_ref, zeta_ref):
    c = q_ref.shape[1]
    n_heads = r_ref.shape[0]
    dk, dv = r_ref.shape[1], r_ref.shape[2]
    half = dk // 2

    @pl.when(pl.program_id(1) == 0)
    def _():
        r_ref[...] = jnp.zeros_like(r_ref)
        rel = (lax.broadcasted_iota(jnp.int32, (c, c), 0)
               - lax.broadcasted_iota(jnp.int32, (c, c), 1)).astype(F32)
        idx = lax.broadcasted_iota(jnp.int32, (c, 1), 0).astype(F32)
        for hd in range(n_heads):
            lg = lg_ref[0, hd]
            dec_ref[hd] = jnp.where(rel >= 0, jnp.exp(jnp.maximum(rel, 0.0) * lg), 0.0)
            xi_ref[hd] = jnp.exp((idx + 1.0) * lg)
            zeta_ref[hd] = jnp.exp((c - 1.0 - idx) * lg)

    cos = cos_ref[...]
    sin = sin_ref[...]

    def rot(t):
        t1, t2 = t[:, :half], t[:, half:]
        return jnp.concatenate([t1 * cos - t2 * sin, t1 * sin + t2 * cos], axis=-1)

    for hd in range(n_heads):
        gamma_c = lg_ref[1, hd]
        qr = rot(q_ref[0, :, hd * dk:(hd + 1) * dk].astype(F32)).astype(BF16)
        kr = rot(k_ref[0, :, hd * dk:(hd + 1) * dk].astype(F32)) * (dk ** -0.5)
        v = v_ref[0, :, hd * dv:(hd + 1) * dv]
        scores = lax.dot_general(qr, kr.astype(BF16), (((1,), (1,)), ((), ())),
                                 preferred_element_type=F32) * dec_ref[hd]
        r_old = r_ref[hd]
        o = jnp.dot(scores.astype(BF16), v, preferred_element_type=F32)
        o = o + jnp.dot(qr, r_old.astype(BF16), preferred_element_type=F32) * xi_ref[hd]
        kz = (kr * zeta_ref[hd]).astype(BF16)
        r_ref[hd] = r_old * gamma_c + lax.dot_general(kz, v, (((0,), (0,)), ((), ())),
                                                      preferred_element_type=F32)
        mu = jnp.mean(o, axis=-1, keepdims=True)
        oc = o - mu
        var = jnp.mean(oc * oc, axis=-1, keepdims=True)
        yn = oc * lax.rsqrt(var + GN_EPS)
        rg = rg_ref[0, :, hd * dv:(hd + 1) * dv].astype(F32)
        o_ref[0, :, hd * dv:(hd + 1) * dv] = (rg * jax.nn.sigmoid(rg) * yn).astype(BF16)


def _retention(p, lg_tab, cos, sin, d_model):
    b, s, _ = p.shape
    dk = d_model // RET_HEADS
    dv = 2 * dk
    qk_w, v_w = RET_HEADS * dk, RET_HEADS * dv
    c = RET_CHUNK
    kq, kk, kv, kg = 0, 1, (2 * qk_w) // v_w, (2 * qk_w) // v_w + 1
    return pl.pallas_call(
        _retention_kernel,
        grid=(b, s // c),
        in_specs=[
            pl.BlockSpec(memory_space=pltpu.SMEM),
            pl.BlockSpec((1, c, qk_w), lambda bi, n: (bi, n, kq)),
            pl.BlockSpec((1, c, qk_w), lambda bi, n: (bi, n, kk)),
            pl.BlockSpec((1, c, v_w), lambda bi, n: (bi, n, kv)),
            pl.BlockSpec((1, c, v_w), lambda bi, n: (bi, n, kg)),
            pl.BlockSpec((c, dk // 2), lambda bi, n: (n, 0)),
            pl.BlockSpec((c, dk // 2), lambda bi, n: (n, 0)),
        ],
        out_specs=pl.BlockSpec((1, c, v_w), lambda bi, n: (bi, n, 0)),
        out_shape=jax.ShapeDtypeStruct((b, s, v_w), BF16),
        scratch_shapes=[
            pltpu.VMEM((RET_HEADS, dk, dv), F32),
            pltpu.VMEM((RET_HEADS, c, c), F32),
            pltpu.VMEM((RET_HEADS, c, 1), F32),
            pltpu.VMEM((RET_HEADS, c, 1), F32),
        ],
        compiler_params=_params(("parallel", "arbitrary"), "retention"),
        name="retention",
    )(lg_tab, p, p, p, p, cos, sin)


def _sb_kernel(q_ref, k_ref, v_ref, u_ref, o_ref, qs_ref, acc_ref, carry_ref):
    t = SB_SUB
    n_sub = SB_Q_SUBS
    rows = 2 * t
    i = pl.program_id(2)
    lane = lax.broadcasted_iota(jnp.int32, (1, LANES), 1)
    first = lane < SB_DIM
    for s in range(n_sub):
        q = q_ref[0, s * t:(s + 1) * t, :] * (SB_DIM ** -0.5)
        zero = jnp.zeros_like(q)
        qs_ref[s * rows:s * rows + t, :] = jnp.where(first, q, zero)
        qs_ref[s * rows + t:(s + 1) * rows, :] = jnp.where(first, zero, q)
    acc_ref[...] = jnp.zeros_like(acc_ref)
    carry_ref[...] = jnp.zeros_like(carry_ref)

    def sub(s):
        return slice(s * rows, (s + 1) * rows)

    def unit(s, j, diagonal=False, check_valid=False):
        start = pl.multiple_of(jnp.maximum(j, 0) * t, t)
        k = k_ref[0, pl.ds(start, t), :]
        v = v_ref[0, pl.ds(start, t), :]
        z = lax.dot_general(qs_ref[sub(s), :], k, (((1,), (1,)), ((), ())),
                            preferred_element_type=F32)
        if diagonal:
            causal = (lax.broadcasted_iota(jnp.int32, (rows, t), 1)
                      < (lax.broadcasted_iota(jnp.int32, (rows, t), 0) & (t - 1)))
            z = jnp.where(causal, z, NEG_BIG)
        if check_valid:
            z = jnp.where(j >= 0, z, NEG_BIG)
        sp = jnp.maximum(z, 0.0) + jnp.log(1.0 + jnp.exp2(jnp.abs(z) * (-LOG2E)))
        s_excl = jnp.dot(sp.astype(BF16), u_ref[...], preferred_element_type=F32)
        carry = carry_ref[sub(s), :]
        x = ((z - sp) - (s_excl + carry)) * LOG2E
        carry_ref[sub(s), :] = carry + jnp.sum(sp, axis=-1, keepdims=True)
        acc_ref[sub(s), :] += jnp.dot(jnp.exp2(x).astype(BF16), v, preferred_element_type=F32)

    base = n_sub * i
    for w in range(SB_STRAIGHT_WAVES):
        for s in range(n_sub):
            unit(s, base + s - w, diagonal=(w == 0), check_valid=(s < w))

    def sweep_rest(group):
        def wave_needed(w):
            need = jnp.bool_(False)
            for s in group:
                unsaturated = jnp.min(carry_ref[sub(s), :]) < SB_SATURATED
                need = jnp.logical_or(need, jnp.logical_and(base + s - w >= 0, unsaturated))
            return need

        def wave(state):
            w, _ = state
            for s in group:
                unit(s, base + s - w, check_valid=True)
            return w + 1, wave_needed(w + 1)

        lax.while_loop(lambda state: state[1], wave,
                       (jnp.int32(SB_STRAIGHT_WAVES), wave_needed(SB_STRAIGHT_WAVES)))

    for first_sub in range(0, n_sub, SB_EXIT_GROUP):
        sweep_rest(range(first_sub, first_sub + SB_EXIT_GROUP))
    for s in range(n_sub):
        o_ref[0, s * t:(s + 1) * t, :] = jnp.where(
            first, acc_ref[s * rows:s * rows + t, :], acc_ref[s * rows + t:(s + 1) * rows, :]).astype(BF16)


def _stick_breaking(p, u2, d_model):
    b, s, _ = p.shape
    tq = SB_Q_SUBS * SB_SUB
    pairs = SB_HEADS * SB_DIM // LANES
    base = 6 * d_model // LANES
    return pl.pallas_call(
        _sb_kernel,
        grid=(b, pairs, s // tq),
        in_specs=[
            pl.BlockSpec((1, tq, LANES), lambda bi, hp, i: (bi, i, base + hp)),
            pl.BlockSpec((1, s, LANES), lambda bi, hp, i: (bi, 0, base + pairs + hp)),
            pl.BlockSpec((1, s, LANES), lambda bi, hp, i: (bi, 0, base + 2 * pairs + hp)),
            pl.BlockSpec(u2.shape, lambda bi, hp, i: (0, 0)),
        ],
        out_specs=pl.BlockSpec((1, tq, LANES), lambda bi, hp, i: (bi, i, hp)),
        out_shape=jax.ShapeDtypeStruct((b, s, SB_HEADS * SB_DIM), BF16),
        scratch_shapes=[
            pltpu.VMEM((2 * tq, LANES), BF16),
            pltpu.VMEM((2 * tq, LANES), F32),
            pltpu.VMEM((2 * tq, 1), F32),
        ],
        compiler_params=_params(("parallel", "parallel", "parallel"), "stick_breaking"),
        name="stick_breaking",
    )(p, p, p, u2)


def _mixout_kernel(yr_ref, ys_ref, ga_ref, gb_ref, x_ref, mod_ref, wr_ref, ws_ref, wm_ref, o_ref):
    ya = jnp.dot(yr_ref[0], wr_ref[...], preferred_element_type=F32)
    yb = jnp.dot(ys_ref[0], ws_ref[...], preferred_element_type=F32)
    merged = (jax.nn.sigmoid(ga_ref[0].astype(F32)) * ya
              + jax.nn.sigmoid(gb_ref[0].astype(F32)) * yb)
    out = jnp.dot(merged.astype(BF16), wm_ref[...], preferred_element_type=F32)
    o_ref[0] = x_ref[0] + mod_ref[0, 2:3, :] * out


def _mix_out(y_ret, y_sb, p, x, mod, w_ret_out, w_sb_out, w_mix_out):
    b, s, d = x.shape
    tm = MIX_OUT_ROWS
    ga_blk = (p.shape[2] - 2 * d) // d
    const = lambda bi, i: (0, 0)
    return pl.pallas_call(
        _mixout_kernel,
        grid=(b, s // tm),
        in_specs=[
            pl.BlockSpec((1, tm, y_ret.shape[2]), lambda bi, i: (bi, i, 0)),
            pl.BlockSpec((1, tm, y_sb.shape[2]), lambda bi, i: (bi, i, 0)),
            pl.BlockSpec((1, tm, d), lambda bi, i: (bi, i, ga_blk)),
            pl.BlockSpec((1, tm, d), lambda bi, i: (bi, i, ga_blk + 1)),
            pl.BlockSpec((1, tm, d), lambda bi, i: (bi, i, 0)),
            pl.BlockSpec((1, N_MOD, d), lambda bi, i: (bi, 0, 0)),
            pl.BlockSpec(w_ret_out.shape, const),
            pl.BlockSpec(w_sb_out.shape, const),
            pl.BlockSpec(w_mix_out.shape, const),
        ],
        out_specs=pl.BlockSpec((1, tm, d), lambda bi, i: (bi, i, 0)),
        out_shape=jax.ShapeDtypeStruct((b, s, d), F32),
        compiler_params=_params(("parallel", "parallel"), "mix_out"),
        name="mix_out",
    )(y_ret, y_sb, p, p, x, mod, w_ret_out, w_sb_out, w_mix_out)


def _mlp_kernel(x_ref, g_ref, mod_ref, wu_ref, wd_ref, fg_ref, o_ref, acc_ref, *, final_norm, tf):
    x = x_ref[0]
    h = _norm_modulate(x, g_ref[...], mod_ref[0, 3:4, :], mod_ref[0, 4:5, :]).astype(BF16)
    for f in range(wu_ref.shape[1] // tf):
        up = jnp.dot(h, wu_ref[:, f * tf:(f + 1) * tf], preferred_element_type=F32)
        u = jnp.square(jnp.maximum(up, 0.0)).astype(BF16)
        part = jnp.dot(u, wd_ref[f * tf:(f + 1) * tf, :], preferred_element_type=F32)
        if f == 0:
            acc_ref[...] = part
        else:
            acc_ref[...] += part
    y = x + mod_ref[0, 5:6, :] * acc_ref[...]
    if final_norm:
        ms = jnp.mean(y * y, axis=-1, keepdims=True)
        y = y * lax.rsqrt(ms + EPS) * fg_ref[...]
    o_ref[0] = y


def _mlp(x, g, mod, w_up, w_down, final_g, final_norm):
    b, s, d = x.shape
    ff = w_up.shape[1]
    tm, tf = MLP_ROWS, MLP_FF_CHUNK
    const = lambda bi, i: (0, 0)
    return pl.pallas_call(
        functools.partial(_mlp_kernel, final_norm=final_norm, tf=tf),
        grid=(b, s // tm),
        in_specs=[
            pl.BlockSpec((1, tm, d), lambda bi, i: (bi, i, 0)),
            pl.BlockSpec((1, d), const),
            pl.BlockSpec((1, N_MOD, d), lambda bi, i: (bi, 0, 0)),
            pl.BlockSpec((d, ff), const, pipeline_mode=pl.Buffered(1)),
            pl.BlockSpec((ff, d), const, pipeline_mode=pl.Buffered(1)),
            pl.BlockSpec((1, d), const),
        ],
        out_specs=pl.BlockSpec((1, tm, d), lambda bi, i: (bi, i, 0)),
        out_shape=jax.ShapeDtypeStruct((b, s, d), F32),
        scratch_shapes=[pltpu.VMEM((tm, d), F32)],
        compiler_params=_params(("parallel", "parallel"), "mlp"),
        name="mlp",
    )(x, g.reshape(1, d), mod, w_up, w_down, final_g.reshape(1, d))


def _rotary_tables(s, dk):
    half = dk // 2
    inv_freq = jnp.power(ROPE_BASE, -jnp.arange(half, dtype=F32) / half)
    ang = jnp.arange(s, dtype=F32)[:, None] * inv_freq[None, :]
    return jnp.cos(ang), jnp.sin(ang)


def _decay_table():
    log_gamma = jnp.log1p(-jnp.power(2.0, -5.0 - jnp.arange(RET_HEADS, dtype=F32)))
    return jnp.stack([log_gamma, jnp.exp(RET_CHUNK * log_gamma)])


def kernel(x, c, norm_mix_g, w_in, w_ret_out, w_sb_out, w_mix_out, norm_mlp_g, w_up, w_down,
           w_ada, b_ada, final_g):
    depth = w_in.shape[0]
    b, s, d = x.shape
    cos, sin = _rotary_tables(s, d // RET_HEADS)
    lg_tab = _decay_table()
    t = SB_SUB
    u = (lax.broadcasted_iota(jnp.int32, (t, t), 0)
         > lax.broadcasted_iota(jnp.int32, (t, t), 1)).astype(BF16)

    mod_all = _ada_modulation(c, w_ada, b_ada).reshape(depth, b, N_MOD, d)
    for l in range(depth):
        mod = mod_all[l]
        p = _in_projection(x, norm_mix_g[l], mod, w_in[l].astype(BF16))
        y_ret = _retention(p, lg_tab, cos, sin, d)
        y_sb = _stick_breaking(p, u, d)
        x = _mix_out(y_ret, y_sb, p, x, mod, w_ret_out[l].astype(BF16),
                     w_sb_out[l].astype(BF16), w_mix_out[l].astype(BF16))
        x = _mlp(x, norm_mlp_g[l], mod, w_up[l].astype(BF16), w_down[l].astype(BF16),
                 final_g, final_norm=(l == depth - 1))
    return x
```

```python
import functools

import jax
import jax.numpy as jnp
from jax import lax
from jax.experimental import pallas as pl
from jax.experimental.pallas import tpu as pltpu

F32 = jnp.float32
BF16 = jnp.bfloat16

RET_HEADS = 4
SB_HEADS = 16
SB_DIM = 64
N_MOD = 6
ROPE_BASE = 10000.0
EPS = 1e-6
GN_EPS = 1e-5

RET_CHUNK = 256
SB_SUB = 256
SB_Q_SUBS = 8
SB_EXIT_GROUP = 4
SB_STRAIGHT_WAVES = 3
NEG_BIG = -1e30
SB_SATURATED = 152.0
SB_SOFTPLUS_CLAMP = 64.0
LOG2E = 1.4426950408889634
LANES = 128
MIB = 1024 * 1024

ADA_COL_TILES = 4
IN_PROJ_ROWS = 1024
IN_PROJ_COL_TILES = 4
MIX_OUT_ROWS = 512
MLP_ROWS = 1024
MLP_FF_CHUNK = 1024
VMEM_LIMIT_MIB = {"ada": 32, "in_proj": 56, "retention": 32, "stick_breaking": 32,
                  "mix_out": 48, "mlp": 56}


def _params(semantics, call):
    return pltpu.CompilerParams(dimension_semantics=semantics,
                                vmem_limit_bytes=VMEM_LIMIT_MIB[call] * MIB)


def _ada_kernel(c_ref, w_ref, b_ref, o_ref):
    c = c_ref[...]
    c_act = c * jax.nn.sigmoid(c)
    o_ref[0] = jnp.dot(c_act, w_ref[0], preferred_element_type=F32) + b_ref[0]


def _ada_modulation(c, w_ada, b_ada):
    depth, d, n = w_ada.shape
    b = c.shape[0]
    tn = n // ADA_COL_TILES
    return pl.pallas_call(
        _ada_kernel,
        grid=(depth, n // tn),
        in_specs=[
            pl.BlockSpec((b, d), lambda l, j: (0, 0)),
            pl.BlockSpec((1, d, tn), lambda l, j: (l, 0, j)),
            pl.BlockSpec((1, 1, tn), lambda l, j: (l, 0, j)),
        ],
        out_specs=pl.BlockSpec((1, b, tn), lambda l, j: (l, 0, j)),
        out_shape=jax.ShapeDtypeStruct((depth, b, n), F32),
        compiler_params=_params(("parallel", "parallel"), "ada"),
        name="ada_modulation",
    )(c, w_ada, b_ada.reshape(depth, 1, n))


def _norm_modulate(x, g, shift, scale):
    ms = jnp.mean(x * x, axis=-1, keepdims=True)
    y = x * lax.rsqrt(ms + EPS) * g
    return y * (1.0 + scale) + shift


def _inproj_kernel(x_ref, g_ref, mod_ref, w_ref, cs_ref, o_ref, h_ref):
    @pl.when(pl.program_id(2) == 0)
    def _():
        h = _norm_modulate(x_ref[0], g_ref[...], mod_ref[0, 0:1, :], mod_ref[0, 1:2, :])
        h_ref[...] = h.astype(BF16)

    o_ref[0] = (jnp.dot(h_ref[...], w_ref[...], preferred_element_type=F32)
                * cs_ref[...]).astype(BF16)


def _in_projection(x, g, mod, w_in, col_scale):
    b, s, d = x.shape
    n = w_in.shape[1]
    tm, tn = IN_PROJ_ROWS, n // IN_PROJ_COL_TILES
    return pl.pallas_call(
        _inproj_kernel,
        grid=(b, s // tm, n // tn),
        in_specs=[
            pl.BlockSpec((1, tm, d), lambda bi, i, j: (bi, i, 0)),
            pl.BlockSpec((1, d), lambda bi, i, j: (0, 0)),
            pl.BlockSpec((1, N_MOD, d), lambda bi, i, j: (bi, 0, 0)),
            pl.BlockSpec((d, tn), lambda bi, i, j: (0, j)),
            pl.BlockSpec((1, tn), lambda bi, i, j: (0, j)),
        ],
        out_specs=pl.BlockSpec((1, tm, tn), lambda bi, i, j: (bi, i, j)),
        out_shape=jax.ShapeDtypeStruct((b, s, n), BF16),
        scratch_shapes=[pltpu.VMEM((tm, d), BF16)],
        compiler_params=_params(("parallel", "parallel", "arbitrary"), "in_proj"),
        name="in_projection",
    )(x, g.reshape(1, d), mod, w_in, col_scale)


def _retention_kernel(lg_ref, q_ref, k_ref, v_ref, rg_ref, cos_ref, sin_ref, o_ref,
                      r_ref, dec_ref, xi_ref, zeta_ref):
    c = q_ref.shape[1]
    n_heads = r_ref.shape[0]
    dk, dv = r_ref.shape[1], r_ref.shape[2]
    half = dk // 2

    @pl.when(pl.program_id(1) == 0)
    def _():
        r_ref[...] = jnp.zeros_like(r_ref)
        rel = (lax.broadcasted_iota(jnp.int32, (c, c), 0)
               - lax.broadcasted_iota(jnp.int32, (c, c), 1)).astype(F32)
        idx = lax.broadcasted_iota(jnp.int32, (c, 1), 0).astype(F32)
        for hd in range(n_heads):
            lg = lg_ref[0, hd]
            dec_ref[hd] = jnp.where(rel >= 0, jnp.exp(jnp.maximum(rel, 0.0) * lg), 0.0)
            xi_ref[hd] = jnp.exp((idx + 1.0) * lg)
            zeta_ref[hd] = jnp.exp((c - 1.0 - idx) * lg)

    cos = cos_ref[...]
    sin = sin_ref[...]

    def rot(t):
        t1, t2 = t[:, :half], t[:, half:]
        return jnp.concatenate([t1 * cos - t2 * sin, t1 * sin + t2 * cos], axis=-1)

    for hd in range(n_heads):
        gamma_c = lg_ref[1, hd]
        qr = rot(q_ref[0, :, hd * dk:(hd + 1) * dk].astype(F32)).astype(BF16)
        kr = rot(k_ref[0, :, hd * dk:(hd + 1) * dk].astype(F32)) * (dk ** -0.5)
        v = v_ref[0, :, hd * dv:(hd + 1) * dv]
        scores = lax.dot_general(qr, kr.astype(BF16), (((1,), (1,)), ((), ())),
                                 preferred_element_type=F32) * dec_ref[hd]
        r_old = r_ref[hd]
        o = jnp.dot(scores.astype(BF16), v, preferred_element_type=F32)
        o = o + jnp.dot(qr, r_old.astype(BF16), preferred_element_type=F32) * xi_ref[hd]
        kz = (kr * zeta_ref[hd]).astype(BF16)
        r_ref[hd] = r_old * gamma_c + lax.dot_general(kz, v, (((0,), (0,)), ((), ())),
                                                      preferred_element_type=F32)
        mu = jnp.mean(o, axis=-1, keepdims=True)
        oc = o - mu
        var = jnp.mean(oc * oc, axis=-1, keepdims=True)
        yn = oc * lax.rsqrt(var + GN_EPS)
        rg = rg_ref[0, :, hd * dv:(hd + 1) * dv].astype(F32)
        o_ref[0, :, hd * dv:(hd + 1) * dv] = (rg * jax.nn.sigmoid(rg) * yn).astype(BF16)


def _retention(p, lg_tab, cos, sin, d_model):
    b, s, _ = p.shape
    dk = d_model // RET_HEADS
    dv = 2 * dk
    qk_w, v_w = RET_HEADS * dk, RET_HEADS * dv
    c = RET_CHUNK
    kq, kk, kv, kg = 0, 1, (2 * qk_w) // v_w, (2 * qk_w) // v_w + 1
    return pl.pallas_call(
        _retention_kernel,
        grid=(b, s // c),
        in_specs=[
            pl.BlockSpec(memory_space=pltpu.SMEM),
            pl.BlockSpec((1, c, qk_w), lambda bi, n: (bi, n, kq)),
            pl.BlockSpec((1, c, qk_w), lambda bi, n: (bi, n, kk)),
            pl.BlockSpec((1, c, v_w), lambda bi, n: (bi, n, kv)),
            pl.BlockSpec((1, c, v_w), lambda bi, n: (bi, n, kg)),
            pl.BlockSpec((c, dk // 2), lambda bi, n: (n, 0)),
            pl.BlockSpec((c, dk // 2), lambda bi, n: (n, 0)),
        ],
        out_specs=pl.BlockSpec((1, c, v_w), lambda bi, n: (bi, n, 0)),
        out_shape=jax.ShapeDtypeStruct((b, s, v_w), BF16),
        scratch_shapes=[
            pltpu.VMEM((RET_HEADS, dk, dv), F32),
            pltpu.VMEM((RET_HEADS, c, c), F32),
            pltpu.VMEM((RET_HEADS, c, 1), F32),
            pltpu.VMEM((RET_HEADS, c, 1), F32),
        ],
        compiler_params=_params(("parallel", "arbitrary"), "retention"),
        name="retention",
    )(lg_tab, p, p, p, p, cos, sin)


def _sb_kernel(q_ref, k_ref, v_ref, u_ref, o_ref, qs_ref, acc_ref, carry_ref):
    t = SB_SUB
    n_sub = SB_Q_SUBS
    rows = 2 * t
    i = pl.program_id(2)
    lane = lax.broadcasted_iota(jnp.int32, (1, LANES), 1)
    first = lane < SB_DIM
    for s in range(n_sub):
        q = q_ref[0, s * t:(s + 1) * t, :]
        zero = jnp.zeros_like(q)
        qs_ref[s * rows:s * rows + t, :] = jnp.where(first, q, zero)
        qs_ref[s * rows + t:(s + 1) * rows, :] = jnp.where(first, zero, q)
    acc_ref[...] = jnp.zeros_like(acc_ref)
    carry_ref[...] = jnp.zeros_like(carry_ref)

    def sub(s):
        return slice(s * rows, (s + 1) * rows)

    def unit(s, j, diagonal=False, check_valid=False):
        start = pl.multiple_of(jnp.maximum(j, 0) * t, t)
        k = k_ref[0, pl.ds(start, t), :]
        v = v_ref[0, pl.ds(start, t), :]
        z = lax.dot_general(qs_ref[sub(s), :], k, (((1,), (1,)), ((), ())),
                            preferred_element_type=F32)
        if diagonal:
            causal = (lax.broadcasted_iota(jnp.int32, (rows, t), 1)
                      < (lax.broadcasted_iota(jnp.int32, (rows, t), 0) & (t - 1)))
            z = jnp.where(causal, z, NEG_BIG)
        if check_valid:
            z = jnp.where(j >= 0, z, NEG_BIG)
        sp = jnp.maximum(z, jnp.log(1.0 + jnp.exp2(jnp.minimum(z, SB_SOFTPLUS_CLAMP))) * LOG2E)
        s_excl = jnp.dot(sp.astype(BF16), u_ref[...], preferred_element_type=F32)
        carry = carry_ref[sub(s), :]
        x = (z - sp) - (s_excl + carry)
        carry_ref[sub(s), :] = carry + jnp.sum(sp, axis=-1, keepdims=True)
        acc_ref[sub(s), :] += jnp.dot(jnp.exp2(x).astype(BF16), v, preferred_element_type=F32)

    base = n_sub * i
    for w in range(SB_STRAIGHT_WAVES):
        for s in range(n_sub):
            unit(s, base + s - w, diagonal=(w == 0), check_valid=(s < w))

    def sweep_rest(group):
        def wave_needed(w):
            need = jnp.bool_(False)
            for s in group:
                unsaturated = jnp.min(carry_ref[sub(s), :]) < SB_SATURATED
                need = jnp.logical_or(need, jnp.logical_and(base + s - w >= 0, unsaturated))
            return need

        def wave(state):
            w, _ = state
            for s in group:
                unit(s, base + s - w, check_valid=True)
            return w + 1, wave_needed(w + 1)

        lax.while_loop(lambda state: state[1], wave,
                       (jnp.int32(SB_STRAIGHT_WAVES), wave_needed(SB_STRAIGHT_WAVES)))

    for first_sub in range(0, n_sub, SB_EXIT_GROUP):
        sweep_rest(range(first_sub, first_sub + SB_EXIT_GROUP))
    for s in range(n_sub):
        o_ref[0, s * t:(s + 1) * t, :] = jnp.where(
            first, acc_ref[s * rows:s * rows + t, :], acc_ref[s * rows + t:(s + 1) * rows, :]).astype(BF16)


def _stick_breaking(p, u2, d_model):
    b, s, _ = p.shape
    tq = SB_Q_SUBS * SB_SUB
    pairs = SB_HEADS * SB_DIM // LANES
    base = 6 * d_model // LANES
    return pl.pallas_call(
        _sb_kernel,
        grid=(b, pairs, s // tq),
        in_specs=[
            pl.BlockSpec((1, tq, LANES), lambda bi, hp, i: (bi, i, base + hp)),
            pl.BlockSpec((1, s, LANES), lambda bi, hp, i: (bi, 0, base + pairs + hp)),
            pl.BlockSpec((1, s, LANES), lambda bi, hp, i: (bi, 0, base + 2 * pairs + hp)),
            pl.BlockSpec(u2.shape, lambda bi, hp, i: (0, 0)),
        ],
        out_specs=pl.BlockSpec((1, tq, LANES), lambda bi, hp, i: (bi, i, hp)),
        out_shape=jax.ShapeDtypeStruct((b, s, SB_HEADS * SB_DIM), BF16),
        scratch_shapes=[
            pltpu.VMEM((2 * tq, LANES), BF16),
            pltpu.VMEM((2 * tq, LANES), F32),
            pltpu.VMEM((2 * tq, 1), F32),
        ],
        compiler_params=_params(("parallel", "parallel", "parallel"), "stick_breaking"),
        name="stick_breaking",
    )(p, p, p, u2)


def _mixout_kernel(yr_ref, ys_ref, ga_ref, gb_ref, x_ref, mod_ref, wr_ref, ws_ref, wm_ref, o_ref):
    ya = jnp.dot(yr_ref[0], wr_ref[...], preferred_element_type=F32)
    yb = jnp.dot(ys_ref[0], ws_ref[...], preferred_element_type=F32)
    merged = (jax.nn.sigmoid(ga_ref[0].astype(F32)) * ya
              + jax.nn.sigmoid(gb_ref[0].astype(F32)) * yb)
    out = jnp.dot(merged.astype(BF16), wm_ref[...], preferred_element_type=F32)
    o_ref[0] = x_ref[0] + mod_ref[0, 2:3, :] * out


def _mix_out(y_ret, y_sb, p, x, mod, w_ret_out, w_sb_out, w_mix_out):
    b, s, d = x.shape
    tm = MIX_OUT_ROWS
    ga_blk = (p.shape[2] - 2 * d) // d
    const = lambda bi, i: (0, 0)
    return pl.pallas_call(
        _mixout_kernel,
        grid=(b, s // tm),
        in_specs=[
            pl.BlockSpec((1, tm, y_ret.shape[2]), lambda bi, i: (bi, i, 0)),
            pl.BlockSpec((1, tm, y_sb.shape[2]), lambda bi, i: (bi, i, 0)),
            pl.BlockSpec((1, tm, d), lambda bi, i: (bi, i, ga_blk)),
            pl.BlockSpec((1, tm, d), lambda bi, i: (bi, i, ga_blk + 1)),
            pl.BlockSpec((1, tm, d), lambda bi, i: (bi, i, 0)),
            pl.BlockSpec((1, N_MOD, d), lambda bi, i: (bi, 0, 0)),
            pl.BlockSpec(w_ret_out.shape, const),
            pl.BlockSpec(w_sb_out.shape, const),
            pl.BlockSpec(w_mix_out.shape, const),
        ],
        out_specs=pl.BlockSpec((1, tm, d), lambda bi, i: (bi, i, 0)),
        out_shape=jax.ShapeDtypeStruct((b, s, d), F32),
        compiler_params=_params(("parallel", "parallel"), "mix_out"),
        name="mix_out",
    )(y_ret, y_sb, p, p, x, mod, w_ret_out, w_sb_out, w_mix_out)


def _mlp_kernel(x_ref, g_ref, mod_ref, wu_ref, wd_ref, fg_ref, o_ref, acc_ref, *, final_norm, tf):
    x = x_ref[0]
    h = _norm_modulate(x, g_ref[...], mod_ref[0, 3:4, :], mod_ref[0, 4:5, :]).astype(BF16)
    for f in range(wu_ref.shape[1] // tf):
        up = jnp.dot(h, wu_ref[:, f * tf:(f + 1) * tf], preferred_element_type=F32)
        u = jnp.square(jnp.maximum(up, 0.0)).astype(BF16)
        part = jnp.dot(u, wd_ref[f * tf:(f + 1) * tf, :], preferred_element_type=F32)
        if f == 0:
            acc_ref[...] = part
        else:
            acc_ref[...] += part
    y = x + mod_ref[0, 5:6, :] * acc_ref[...]
    if final_norm:
        ms = jnp.mean(y * y, axis=-1, keepdims=True)
        y = y * lax.rsqrt(ms + EPS) * fg_ref[...]
    o_ref[0] = y


def _mlp(x, g, mod, w_up, w_down, final_g, final_norm):
    b, s, d = x.shape
    ff = w_up.shape[1]
    tm, tf = MLP_ROWS, MLP_FF_CHUNK
    const = lambda bi, i: (0, 0)
    return pl.pallas_call(
        functools.partial(_mlp_kernel, final_norm=final_norm, tf=tf),
        grid=(b, s // tm),
        in_specs=[
            pl.BlockSpec((1, tm, d), lambda bi, i: (bi, i, 0)),
            pl.BlockSpec((1, d), const),
            pl.BlockSpec((1, N_MOD, d), lambda bi, i: (bi, 0, 0)),
            pl.BlockSpec((d, ff), const, pipeline_mode=pl.Buffered(1)),
            pl.BlockSpec((ff, d), const, pipeline_mode=pl.Buffered(1)),
            pl.BlockSpec((1, d), const),
        ],
        out_specs=pl.BlockSpec((1, tm, d), lambda bi, i: (bi, i, 0)),
        out_shape=jax.ShapeDtypeStruct((b, s, d), F32),
        scratch_shapes=[pltpu.VMEM((tm, d), F32)],
        compiler_params=_params(("parallel", "parallel"), "mlp"),
        name="mlp",
    )(x, g.reshape(1, d), mod, w_up, w_down, final_g.reshape(1, d))


def _rotary_tables(s, dk):
    half = dk // 2
    inv_freq = jnp.power(ROPE_BASE, -jnp.arange(half, dtype=F32) / half)
    ang = jnp.arange(s, dtype=F32)[:, None] * inv_freq[None, :]
    return jnp.cos(ang), jnp.sin(ang)


def _decay_table():
    log_gamma = jnp.log1p(-jnp.power(2.0, -5.0 - jnp.arange(RET_HEADS, dtype=F32)))
    return jnp.stack([log_gamma, jnp.exp(RET_CHUNK * log_gamma)])


def _projection_column_scale(d_model, n):
    col = jnp.arange(n)
    is_sb_q = (col >= 6 * d_model) & (col < 6 * d_model + SB_HEADS * SB_DIM)
    return jnp.where(is_sb_q, LOG2E * SB_DIM ** -0.5, 1.0).astype(F32).reshape(1, n)


def kernel(x, c, norm_mix_g, w_in, w_ret_out, w_sb_out, w_mix_out, norm_mlp_g, w_up, w_down,
           w_ada, b_ada, final_g):
    depth = w_in.shape[0]
    b, s, d = x.shape
    cos, sin = _rotary_tables(s, d // RET_HEADS)
    lg_tab = _decay_table()
    col_scale = _projection_column_scale(d, w_in.shape[2])
    t = SB_SUB
    u = (lax.broadcasted_iota(jnp.int32, (t, t), 0)
         > lax.broadcasted_iota(jnp.int32, (t, t), 1)).astype(BF16)

    mod_all = _ada_modulation(c, w_ada, b_ada).reshape(depth, b, N_MOD, d)
    for l in range(depth):
        mod = mod_all[l]
        p = _in_projection(x, norm_mix_g[l], mod, w_in[l].astype(BF16), col_scale)
        y_ret = _retention(p, lg_tab, cos, sin, d)
        y_sb = _stick_breaking(p, u, d)
        x = _mix_out(y_ret, y_sb, p, x, mod, w_ret_out[l].astype(BF16),
                     w_sb_out[l].astype(BF16), w_mix_out[l].astype(BF16))
        x = _mlp(x, norm_mlp_g[l], mod, w_up[l].astype(BF16), w_down[l].astype(BF16),
                 final_g, final_norm=(l == depth - 1))
    return x
```

```python
import functools

import jax
import jax.numpy as jnp
from jax import lax
from jax.experimental import pallas as pl
from jax.experimental.pallas import tpu as pltpu

F32 = jnp.float32
BF16 = jnp.bfloat16

RET_HEADS = 4
SB_HEADS = 16
SB_DIM = 64
N_MOD = 6
ROPE_BASE = 10000.0
EPS = 1e-6
GN_EPS = 1e-5

RET_CHUNK = 256
SB_SUB = 256
SB_Q_SUBS = 8
SB_EXIT_GROUP = 2
SB_STRAIGHT_WAVES = 3
NEG_BIG = -1e30
SB_SATURATED = 152.0
SB_SOFTPLUS_CLAMP = 64.0
LOG2E = 1.4426950408889634
LANES = 128
MIB = 1024 * 1024

ADA_COL_TILES = 4
IN_PROJ_ROWS = 1024
IN_PROJ_COL_TILES = 4
MIX_OUT_ROWS = 1024
MLP_ROWS = 1024
MLP_FF_CHUNK = 1024
VMEM_LIMIT_MIB = {"ada": 32, "in_proj": 56, "retention": 32, "stick_breaking": 32,
                  "mix_out": 56, "mlp": 56}


def _params(semantics, call):
    return pltpu.CompilerParams(dimension_semantics=semantics,
                                vmem_limit_bytes=VMEM_LIMIT_MIB[call] * MIB)


def _ada_kernel(c_ref, w_ref, b_ref, o_ref):
    c = c_ref[...]
    c_act = c * jax.nn.sigmoid(c)
    o_ref[0] = jnp.dot(c_act, w_ref[0], preferred_element_type=F32) + b_ref[0]


def _ada_modulation(c, w_ada, b_ada):
    depth, d, n = w_ada.shape
    b = c.shape[0]
    tn = n // ADA_COL_TILES
    return pl.pallas_call(
        _ada_kernel,
        grid=(depth, n // tn),
        in_specs=[
            pl.BlockSpec((b, d), lambda l, j: (0, 0)),
            pl.BlockSpec((1, d, tn), lambda l, j: (l, 0, j)),
            pl.BlockSpec((1, 1, tn), lambda l, j: (l, 0, j)),
        ],
        out_specs=pl.BlockSpec((1, b, tn), lambda l, j: (l, 0, j)),
        out_shape=jax.ShapeDtypeStruct((depth, b, n), F32),
        compiler_params=_params(("parallel", "parallel"), "ada"),
        name="ada_modulation",
    )(c, w_ada, b_ada.reshape(depth, 1, n))


def _norm_modulate(x, g, shift, scale):
    ms = jnp.mean(x * x, axis=-1, keepdims=True)
    y = x * lax.rsqrt(ms + EPS) * g
    return y * (1.0 + scale) + shift


def _inproj_kernel(x_ref, g_ref, mod_ref, w_ref, cs_ref, o_ref, h_ref):
    @pl.when(pl.program_id(2) == 0)
    def _():
        h = _norm_modulate(x_ref[0], g_ref[...], mod_ref[0, 0:1, :], mod_ref[0, 1:2, :])
        h_ref[...] = h.astype(BF16)

    o_ref[0] = (jnp.dot(h_ref[...], w_ref[...], preferred_element_type=F32)
                * cs_ref[...]).astype(BF16)


def _in_projection(x, g, mod, w_in, col_scale):
    b, s, d = x.shape
    n = w_in.shape[1]
    tm, tn = IN_PROJ_ROWS, n // IN_PROJ_COL_TILES
    return pl.pallas_call(
        _inproj_kernel,
        grid=(b, s // tm, n // tn),
        in_specs=[
            pl.BlockSpec((1, tm, d), lambda bi, i, j: (bi, i, 0)),
            pl.BlockSpec((1, d), lambda bi, i, j: (0, 0)),
            pl.BlockSpec((1, N_MOD, d), lambda bi, i, j: (bi, 0, 0)),
            pl.BlockSpec((d, tn), lambda bi, i, j: (0, j)),
            pl.BlockSpec((1, tn), lambda bi, i, j: (0, j)),
        ],
        out_specs=pl.BlockSpec((1, tm, tn), lambda bi, i, j: (bi, i, j)),
        out_shape=jax.ShapeDtypeStruct((b, s, n), BF16),
        scratch_shapes=[pltpu.VMEM((tm, d), BF16)],
        compiler_params=_params(("parallel", "parallel", "arbitrary"), "in_proj"),
        name="in_projection",
    )(x, g.reshape(1, d), mod, w_in, col_scale)


def _retention_kernel(lg_ref, q_ref, k_ref, v_ref, rg_ref, cos_ref, sin_ref, o_ref,
                      r_ref, dec_ref, xi_ref, zeta_ref):
    c = q_ref.shape[1]
    n_heads = r_ref.shape[0]
    dk, dv = r_ref.shape[1], r_ref.shape[2]
    half = dk // 2

    @pl.when(pl.program_id(1) == 0)
    def _():
        r_ref[...] = jnp.zeros_like(r_ref)
        rel = (lax.broadcasted_iota(jnp.int32, (c, c), 0)
               - lax.broadcasted_iota(jnp.int32, (c, c), 1)).astype(F32)
        idx = lax.broadcasted_iota(jnp.int32, (c, 1), 0).astype(F32)
        for hd in range(n_heads):
            lg = lg_ref[0, hd]
            dec_ref[hd] = jnp.where(rel >= 0, jnp.exp(jnp.maximum(rel, 0.0) * lg), 0.0)
            xi_ref[hd] = jnp.exp((idx + 1.0) * lg)
            zeta_ref[hd] = jnp.exp((c - 1.0 - idx) * lg)

    cos = cos_ref[...]
    sin = sin_ref[...]

    def rot(t):
        t1, t2 = t[:, :half], t[:, half:]
        return jnp.concatenate([t1 * cos - t2 * sin, t1 * sin + t2 * cos], axis=-1)

    for hd in range(n_heads):
        gamma_c = lg_ref[1, hd]
        qr = rot(q_ref[0, :, hd * dk:(hd + 1) * dk].astype(F32)).astype(BF16)
        kr = rot(k_ref[0, :, hd * dk:(hd + 1) * dk].astype(F32)) * (dk ** -0.5)
        v = v_ref[0, :, hd * dv:(hd + 1) * dv]
        scores = lax.dot_general(qr, kr.astype(BF16), (((1,), (1,)), ((), ())),
                                 preferred_element_type=F32) * dec_ref[hd]
        r_old = r_ref[hd]
        o = jnp.dot(scores.astype(BF16), v, preferred_element_type=F32)
        o = o + jnp.dot(qr, r_old.astype(BF16), preferred_element_type=F32) * xi_ref[hd]
        kz = (kr * zeta_ref[hd]).astype(BF16)
        r_ref[hd] = r_old * gamma_c + lax.dot_general(kz, v, (((0,), (0,)), ((), ())),
                                                      preferred_element_type=F32)
        mu = jnp.mean(o, axis=-1, keepdims=True)
        oc = o - mu
        var = jnp.mean(oc * oc, axis=-1, keepdims=True)
        yn = oc * lax.rsqrt(var + GN_EPS)
        rg = rg_ref[0, :, hd * dv:(hd + 1) * dv].astype(F32)
        o_ref[0, :, hd * dv:(hd + 1) * dv] = (rg * jax.nn.sigmoid(rg) * yn).astype(BF16)


def _retention(p, lg_tab, cos, sin, d_model):
    b, s, _ = p.shape
    dk = d_model // RET_HEADS
    dv = 2 * dk
    qk_w, v_w = RET_HEADS * dk, RET_HEADS * dv
    c = RET_CHUNK
    kq, kk, kv, kg = 0, 1, (2 * qk_w) // v_w, (2 * qk_w) // v_w + 1
    return pl.pallas_call(
        _retention_kernel,
        grid=(b, s // c),
        in_specs=[
            pl.BlockSpec(memory_space=pltpu.SMEM),
            pl.BlockSpec((1, c, qk_w), lambda bi, n: (bi, n, kq)),
            pl.BlockSpec((1, c, qk_w), lambda bi, n: (bi, n, kk)),
            pl.BlockSpec((1, c, v_w), lambda bi, n: (bi, n, kv)),
            pl.BlockSpec((1, c, v_w), lambda bi, n: (bi, n, kg)),
            pl.BlockSpec((c, dk // 2), lambda bi, n: (n, 0)),
            pl.BlockSpec((c, dk // 2), lambda bi, n: (n, 0)),
        ],
        out_specs=pl.BlockSpec((1, c, v_w), lambda bi, n: (bi, n, 0)),
        out_shape=jax.ShapeDtypeStruct((b, s, v_w), BF16),
        scratch_shapes=[
            pltpu.VMEM((RET_HEADS, dk, dv), F32),
            pltpu.VMEM((RET_HEADS, c, c), F32),
            pltpu.VMEM((RET_HEADS, c, 1), F32),
            pltpu.VMEM((RET_HEADS, c, 1), F32),
        ],
        compiler_params=_params(("parallel", "arbitrary"), "retention"),
        name="retention",
    )(lg_tab, p, p, p, p, cos, sin)


def _sb_kernel(q_ref, k_ref, v_ref, u_ref, o_ref, qs_ref, acc_ref, carry_ref):
    t = SB_SUB
    n_sub = SB_Q_SUBS
    rows = 2 * t
    i = pl.program_id(2)
    lane = lax.broadcasted_iota(jnp.int32, (1, LANES), 1)
    first = lane < SB_DIM
    for s in range(n_sub):
        q = q_ref[0, s * t:(s + 1) * t, :]
        zero = jnp.zeros_like(q)
        qs_ref[s * rows:s * rows + t, :] = jnp.where(first, q, zero)
        qs_ref[s * rows + t:(s + 1) * rows, :] = jnp.where(first, zero, q)
    acc_ref[...] = jnp.zeros_like(acc_ref)
    carry_ref[...] = jnp.zeros_like(carry_ref)

    def sub(s):
        return slice(s * rows, (s + 1) * rows)

    def unit(s, j, diagonal=False, check_valid=False):
        start = pl.multiple_of(jnp.maximum(j, 0) * t, t)
        k = k_ref[0, pl.ds(start, t), :]
        v = v_ref[0, pl.ds(start, t), :]
        z = lax.dot_general(qs_ref[sub(s), :], k, (((1,), (1,)), ((), ())),
                            preferred_element_type=F32)
        if diagonal:
            causal = (lax.broadcasted_iota(jnp.int32, (rows, t), 1)
                      < (lax.broadcasted_iota(jnp.int32, (rows, t), 0) & (t - 1)))
            z = jnp.where(causal, z, NEG_BIG)
        if check_valid:
            z = jnp.where(j >= 0, z, NEG_BIG)
        sp = jnp.maximum(z, jnp.log(1.0 + jnp.exp2(jnp.minimum(z, SB_SOFTPLUS_CLAMP))) * LOG2E)
        s_excl = jnp.dot(sp.astype(BF16), u_ref[...], preferred_element_type=F32)
        carry = carry_ref[sub(s), :]
        x = (z - sp) - (s_excl + carry)
        carry_ref[sub(s), :] = carry + jnp.sum(sp, axis=-1, keepdims=True)
        acc_ref[sub(s), :] += jnp.dot(jnp.exp2(x).astype(BF16), v, preferred_element_type=F32)

    base = n_sub * i
    for w in range(SB_STRAIGHT_WAVES):
        for s in range(n_sub):
            unit(s, base + s - w, diagonal=(w == 0), check_valid=(s < w))

    def sweep_rest(group):
        def wave_needed(w):
            need = jnp.bool_(False)
            for s in group:
                unsaturated = jnp.min(carry_ref[sub(s), :]) < SB_SATURATED
                need = jnp.logical_or(need, jnp.logical_and(base + s - w >= 0, unsaturated))
            return need

        def wave(state):
            w, _ = state
            for s in group:
                unit(s, base + s - w, check_valid=True)
            return w + 1, wave_needed(w + 1)

        lax.while_loop(lambda state: state[1], wave,
                       (jnp.int32(SB_STRAIGHT_WAVES), wave_needed(SB_STRAIGHT_WAVES)))

    for first_sub in range(0, n_sub, SB_EXIT_GROUP):
        sweep_rest(range(first_sub, first_sub + SB_EXIT_GROUP))
    for s in range(n_sub):
        o_ref[0, s * t:(s + 1) * t, :] = jnp.where(
            first, acc_ref[s * rows:s * rows + t, :], acc_ref[s * rows + t:(s + 1) * rows, :]).astype(BF16)


def _stick_breaking(p, u2, d_model):
    b, s, _ = p.shape
    tq = SB_Q_SUBS * SB_SUB
    pairs = SB_HEADS * SB_DIM // LANES
    base = 6 * d_model // LANES
    return pl.pallas_call(
        _sb_kernel,
        grid=(b, pairs, s // tq),
        in_specs=[
            pl.BlockSpec((1, tq, LANES), lambda bi, hp, i: (bi, i, base + hp)),
            pl.BlockSpec((1, s, LANES), lambda bi, hp, i: (bi, 0, base + pairs + hp)),
            pl.BlockSpec((1, s, LANES), lambda bi, hp, i: (bi, 0, base + 2 * pairs + hp)),
            pl.BlockSpec(u2.shape, lambda bi, hp, i: (0, 0)),
        ],
        out_specs=pl.BlockSpec((1, tq, LANES), lambda bi, hp, i: (bi, i, hp)),
        out_shape=jax.ShapeDtypeStruct((b, s, SB_HEADS * SB_DIM), BF16),
        scratch_shapes=[
            pltpu.VMEM((2 * tq, LANES), BF16),
            pltpu.VMEM((2 * tq, LANES), F32),
            pltpu.VMEM((2 * tq, 1), F32),
        ],
        compiler_params=_params(("parallel", "parallel", "parallel"), "stick_breaking"),
        name="stick_breaking",
    )(p, p, p, u2)


def _mixout_kernel(yr_ref, ys_ref, ga_ref, gb_ref, x_ref, mod_ref, wr_ref, ws_ref, wm_ref, o_ref):
    ya = jnp.dot(yr_ref[0], wr_ref[...], preferred_element_type=F32)
    yb = jnp.dot(ys_ref[0], ws_ref[...], preferred_element_type=F32)
    merged = (jax.nn.sigmoid(ga_ref[0].astype(F32)) * ya
              + jax.nn.sigmoid(gb_ref[0].astype(F32)) * yb)
    out = jnp.dot(merged.astype(BF16), wm_ref[...], preferred_element_type=F32)
    o_ref[0] = x_ref[0] + mod_ref[0, 2:3, :] * out


def _mix_out(y_ret, y_sb, p, x, mod, w_ret_out, w_sb_out, w_mix_out):
    b, s, d = x.shape
    tm = MIX_OUT_ROWS
    ga_blk = (p.shape[2] - 2 * d) // d
    const = lambda bi, i: (0, 0)
    return pl.pallas_call(
        _mixout_kernel,
        grid=(b, s // tm),
        in_specs=[
            pl.BlockSpec((1, tm, y_ret.shape[2]), lambda bi, i: (bi, i, 0)),
            pl.BlockSpec((1, tm, y_sb.shape[2]), lambda bi, i: (bi, i, 0)),
            pl.BlockSpec((1, tm, d), lambda bi, i: (bi, i, ga_blk)),
            pl.BlockSpec((1, tm, d), lambda bi, i: (bi, i, ga_blk + 1)),
            pl.BlockSpec((1, tm, d), lambda bi, i: (bi, i, 0)),
            pl.BlockSpec((1, N_MOD, d), lambda bi, i: (bi, 0, 0)),
            pl.BlockSpec(w_ret_out.shape, const, pipeline_mode=pl.Buffered(1)),
            pl.BlockSpec(w_sb_out.shape, const, pipeline_mode=pl.Buffered(1)),
            pl.BlockSpec(w_mix_out.shape, const, pipeline_mode=pl.Buffered(1)),
        ],
        out_specs=pl.BlockSpec((1, tm, d), lambda bi, i: (bi, i, 0)),
        out_shape=jax.ShapeDtypeStruct((b, s, d), F32),
        compiler_params=_params(("parallel", "parallel"), "mix_out"),
        name="mix_out",
    )(y_ret, y_sb, p, p, x, mod, w_ret_out, w_sb_out, w_mix_out)


def _mlp_kernel(x_ref, g_ref, mod_ref, wu_ref, wd_ref, fg_ref, o_ref, acc_ref, *, final_norm, tf):
    x = x_ref[0]
    h = _norm_modulate(x, g_ref[...], mod_ref[0, 3:4, :], mod_ref[0, 4:5, :]).astype(BF16)
    for f in range(wu_ref.shape[1] // tf):
        up = jnp.dot(h, wu_ref[:, f * tf:(f + 1) * tf], preferred_element_type=F32)
        u = jnp.square(jnp.maximum(up, 0.0)).astype(BF16)
        part = jnp.dot(u, wd_ref[f * tf:(f + 1) * tf, :], preferred_element_type=F32)
        if f == 0:
            acc_ref[...] = part
        else:
            acc_ref[...] += part
    y = x + mod_ref[0, 5:6, :] * acc_ref[...]
    if final_norm:
        ms = jnp.mean(y * y, axis=-1, keepdims=True)
        y = y * lax.rsqrt(ms + EPS) * fg_ref[...]
    o_ref[0] = y


def _mlp(x, g, mod, w_up, w_down, final_g, final_norm):
    b, s, d = x.shape
    ff = w_up.shape[1]
    tm, tf = MLP_ROWS, MLP_FF_CHUNK
    const = lambda bi, i: (0, 0)
    return pl.pallas_call(
        functools.partial(_mlp_kernel, final_norm=final_norm, tf=tf),
        grid=(b, s // tm),
        in_specs=[
            pl.BlockSpec((1, tm, d), lambda bi, i: (bi, i, 0)),
            pl.BlockSpec((1, d), const),
            pl.BlockSpec((1, N_MOD, d), lambda bi, i: (bi, 0, 0)),
            pl.BlockSpec((d, ff), const, pipeline_mode=pl.Buffered(1)),
            pl.BlockSpec((ff, d), const, pipeline_mode=pl.Buffered(1)),
            pl.BlockSpec((1, d), const),
        ],
        out_specs=pl.BlockSpec((1, tm, d), lambda bi, i: (bi, i, 0)),
        out_shape=jax.ShapeDtypeStruct((b, s, d), F32),
        scratch_shapes=[pltpu.VMEM((tm, d), F32)],
        compiler_params=_params(("parallel", "parallel"), "mlp"),
        name="mlp",
    )(x, g.reshape(1, d), mod, w_up, w_down, final_g.reshape(1, d))


def _rotary_tables(s, dk):
    half = dk // 2
    inv_freq = jnp.power(ROPE_BASE, -jnp.arange(half, dtype=F32) / half)
    ang = jnp.arange(s, dtype=F32)[:, None] * inv_freq[None, :]
    return jnp.cos(ang), jnp.sin(ang)


def _decay_table():
    log_gamma = jnp.log1p(-jnp.power(2.0, -5.0 - jnp.arange(RET_HEADS, dtype=F32)))
    return jnp.stack([log_gamma, jnp.exp(RET_CHUNK * log_gamma)])


def _projection_column_scale(d_model, n):
    col = jnp.arange(n)
    is_sb_q = (col >= 6 * d_model) & (col < 6 * d_model + SB_HEADS * SB_DIM)
    return jnp.where(is_sb_q, LOG2E * SB_DIM ** -0.5, 1.0).astype(F32).reshape(1, n)


def kernel(x, c, norm_mix_g, w_in, w_ret_out, w_sb_out, w_mix_out, norm_mlp_g, w_up, w_down,
           w_ada, b_ada, final_g):
    depth = w_in.shape[0]
    b, s, d = x.shape
    cos, sin = _rotary_tables(s, d // RET_HEADS)
    lg_tab = _decay_table()
    col_scale = _projection_column_scale(d, w_in.shape[2])
    t = SB_SUB
    u = (lax.broadcasted_iota(jnp.int32, (t, t), 0)
         > lax.broadcasted_iota(jnp.int32, (t, t), 1)).astype(BF16)

    mod_all = _ada_modulation(c, w_ada, b_ada).reshape(depth, b, N_MOD, d)
    for l in range(depth):
        mod = mod_all[l]
        p = _in_projection(x, norm_mix_g[l], mod, w_in[l].astype(BF16), col_scale)
        y_ret = _retention(p, lg_tab, cos, sin, d)
        y_sb = _stick_breaking(p, u, d)
        x = _mix_out(y_ret, y_sb, p, x, mod, w_ret_out[l].astype(BF16),
                     w_sb_out[l].astype(BF16), w_mix_out[l].astype(BF16))
        x = _mlp(x, norm_mlp_g[l], mod, w_up[l].astype(BF16), w_down[l].astype(BF16),
                 final_g, final_norm=(l == depth - 1))
    return x
```

```python
import functools

import jax
import jax.numpy as jnp
from jax import lax
from jax.experimental import pallas as pl
from jax.experimental.pallas import tpu as pltpu

F32 = jnp.float32
BF16 = jnp.bfloat16

RET_HEADS = 4
SB_HEADS = 16
SB_DIM = 64
N_MOD = 6
ROPE_BASE = 10000.0
EPS = 1e-6
GN_EPS = 1e-5

RET_CHUNK = 256
SB_SUB = 256
SB_Q_SUBS = 8
SB_EXIT_GROUP = 2
SB_STRAIGHT_WAVES = 3
NEG_BIG = -1e30
SB_SATURATED = 152.0
SB_SOFTPLUS_CLAMP = 64.0
LOG2E = 1.4426950408889634
LANES = 128
MIB = 1024 * 1024

ADA_COL_TILES = 4
IN_PROJ_ROWS = 1024
IN_PROJ_COL_TILES = 4
MIX_OUT_ROWS = 512
MLP_ROWS = 1024
MLP_FF_CHUNK = 1024
VMEM_LIMIT_MIB = {"ada": 32, "in_proj": 56, "retention": 32, "stick_breaking": 32,
                  "mix_out": 40, "mlp": 56}


def _params(semantics, call):
    return pltpu.CompilerParams(dimension_semantics=semantics,
                                vmem_limit_bytes=VMEM_LIMIT_MIB[call] * MIB)


def _ada_kernel(c_ref, w_ref, b_ref, o_ref):
    c = c_ref[...]
    c_act = c * jax.nn.sigmoid(c)
    o_ref[0] = jnp.dot(c_act, w_ref[0], preferred_element_type=F32) + b_ref[0]


def _ada_modulation(c, w_ada, b_ada):
    depth, d, n = w_ada.shape
    b = c.shape[0]
    tn = n // ADA_COL_TILES
    return pl.pallas_call(
        _ada_kernel,
        grid=(depth, n // tn),
        in_specs=[
            pl.BlockSpec((b, d), lambda l, j: (0, 0)),
            pl.BlockSpec((1, d, tn), lambda l, j: (l, 0, j)),
            pl.BlockSpec((1, 1, tn), lambda l, j: (l, 0, j)),
        ],
        out_specs=pl.BlockSpec((1, b, tn), lambda l, j: (l, 0, j)),
        out_shape=jax.ShapeDtypeStruct((depth, b, n), F32),
        compiler_params=_params(("parallel", "parallel"), "ada"),
        name="ada_modulation",
    )(c, w_ada, b_ada.reshape(depth, 1, n))


def _norm_modulate(x, g, shift, scale):
    ms = jnp.mean(x * x, axis=-1, keepdims=True)
    y = x * lax.rsqrt(ms + EPS) * g
    return y * (1.0 + scale) + shift


def _inproj_kernel(x_ref, g_ref, mod_ref, w_ref, cs_ref, o_ref, h_ref):
    @pl.when(pl.program_id(2) == 0)
    def _():
        h = _norm_modulate(x_ref[0], g_ref[...], mod_ref[0, 0:1, :], mod_ref[0, 1:2, :])
        h_ref[...] = h.astype(BF16)

    o_ref[0] = (jnp.dot(h_ref[...], w_ref[...], preferred_element_type=F32)
                * cs_ref[...]).astype(BF16)


def _in_projection(x, g, mod, w_in, col_scale):
    b, s, d = x.shape
    n = w_in.shape[1]
    tm, tn = IN_PROJ_ROWS, n // IN_PROJ_COL_TILES
    return pl.pallas_call(
        _inproj_kernel,
        grid=(b, s // tm, n // tn),
        in_specs=[
            pl.BlockSpec((1, tm, d), lambda bi, i, j: (bi, i, 0)),
            pl.BlockSpec((1, d), lambda bi, i, j: (0, 0)),
            pl.BlockSpec((1, N_MOD, d), lambda bi, i, j: (bi, 0, 0)),
            pl.BlockSpec((d, tn), lambda bi, i, j: (0, j)),
            pl.BlockSpec((1, tn), lambda bi, i, j: (0, j)),
        ],
        out_specs=pl.BlockSpec((1, tm, tn), lambda bi, i, j: (bi, i, j)),
        out_shape=jax.ShapeDtypeStruct((b, s, n), BF16),
        scratch_shapes=[pltpu.VMEM((tm, d), BF16)],
        compiler_params=_params(("parallel", "parallel", "arbitrary"), "in_proj"),
        name="in_projection",
    )(x, g.reshape(1, d), mod, w_in, col_scale)


def _retention_kernel(lg_ref, q_ref, k_ref, v_ref, rg_ref, cos_ref, sin_ref, o_ref,
                      r_ref, dec_ref, xi_ref, zeta_ref):
    c = q_ref.shape[1]
    n_heads = r_ref.shape[0]
    dk, dv = r_ref.shape[1], r_ref.shape[2]
    half = dk // 2

    @pl.when(pl.program_id(1) == 0)
    def _():
        r_ref[...] = jnp.zeros_like(r_ref)
        rel = (lax.broadcasted_iota(jnp.int32, (c, c), 0)
               - lax.broadcasted_iota(jnp.int32, (c, c), 1)).astype(F32)
        idx = lax.broadcasted_iota(jnp.int32, (c, 1), 0).astype(F32)
        for hd in range(n_heads):
            lg = lg_ref[0, hd]
            dec_ref[hd] = jnp.where(rel >= 0, jnp.exp(jnp.maximum(rel, 0.0) * lg), 0.0)
            xi_ref[hd] = jnp.exp((idx + 1.0) * lg)
            zeta_ref[hd] = jnp.exp((c - 1.0 - idx) * lg)

    cos = cos_ref[...]
    sin = sin_ref[...]

    def rot(t):
        t1, t2 = t[:, :half], t[:, half:]
        return jnp.concatenate([t1 * cos - t2 * sin, t1 * sin + t2 * cos], axis=-1)

    for hd in range(n_heads):
        gamma_c = lg_ref[1, hd]
        qr = rot(q_ref[0, :, hd * dk:(hd + 1) * dk].astype(F32)).astype(BF16)
        kr = rot(k_ref[0, :, hd * dk:(hd + 1) * dk].astype(F32)) * (dk ** -0.5)
        v = v_ref[0, :, hd * dv:(hd + 1) * dv]
        scores = lax.dot_general(qr, kr.astype(BF16), (((1,), (1,)), ((), ())),
                                 preferred_element_type=F32) * dec_ref[hd]
        r_old = r_ref[hd]
        o = jnp.dot(scores.astype(BF16), v, preferred_element_type=F32)
        o = o + jnp.dot(qr, r_old.astype(BF16), preferred_element_type=F32) * xi_ref[hd]
        kz = (kr * zeta_ref[hd]).astype(BF16)
        r_ref[hd] = r_old * gamma_c + lax.dot_general(kz, v, (((0,), (0,)), ((), ())),
                                                      preferred_element_type=F32)
        mu = jnp.mean(o, axis=-1, keepdims=True)
        oc = o - mu
        var = jnp.mean(oc * oc, axis=-1, keepdims=True)
        yn = oc * lax.rsqrt(var + GN_EPS)
        rg = rg_ref[0, :, hd * dv:(hd + 1) * dv].astype(F32)
        o_ref[0, :, hd * dv:(hd + 1) * dv] = (rg * jax.nn.sigmoid(rg) * yn).astype(BF16)


def _retention(p, lg_tab, cos, sin, d_model):
    b, s, _ = p.shape
    dk = d_model // RET_HEADS
    dv = 2 * dk
    qk_w, v_w = RET_HEADS * dk, RET_HEADS * dv
    c = RET_CHUNK
    kq, kk, kv, kg = 0, 1, (2 * qk_w) // v_w, (2 * qk_w) // v_w + 1
    return pl.pallas_call(
        _retention_kernel,
        grid=(b, s // c),
        in_specs=[
            pl.BlockSpec(memory_space=pltpu.SMEM),
            pl.BlockSpec((1, c, qk_w), lambda bi, n: (bi, n, kq)),
            pl.BlockSpec((1, c, qk_w), lambda bi, n: (bi, n, kk)),
            pl.BlockSpec((1, c, v_w), lambda bi, n: (bi, n, kv)),
            pl.BlockSpec((1, c, v_w), lambda bi, n: (bi, n, kg)),
            pl.BlockSpec((c, dk // 2), lambda bi, n: (n, 0)),
            pl.BlockSpec((c, dk // 2), lambda bi, n: (n, 0)),
        ],
        out_specs=pl.BlockSpec((1, c, v_w), lambda bi, n: (bi, n, 0)),
        out_shape=jax.ShapeDtypeStruct((b, s, v_w), BF16),
        scratch_shapes=[
            pltpu.VMEM((RET_HEADS, dk, dv), F32),
            pltpu.VMEM((RET_HEADS, c, c), F32),
            pltpu.VMEM((RET_HEADS, c, 1), F32),
            pltpu.VMEM((RET_HEADS, c, 1), F32),
        ],
        compiler_params=_params(("parallel", "arbitrary"), "retention"),
        name="retention",
    )(lg_tab, p, p, p, p, cos, sin)


def _sb_kernel(q_ref, k_ref, v_ref, u_ref, o_ref, qs_ref, acc_ref, carry_ref):
    t = SB_SUB
    n_sub = SB_Q_SUBS
    rows = 2 * t
    i = pl.program_id(2)
    lane = lax.broadcasted_iota(jnp.int32, (1, LANES), 1)
    first = lane < SB_DIM
    for s in range(n_sub):
        q = q_ref[0, s * t:(s + 1) * t, :]
        zero = jnp.zeros_like(q)
        qs_ref[s * rows:s * rows + t, :] = jnp.where(first, q, zero)
        qs_ref[s * rows + t:(s + 1) * rows, :] = jnp.where(first, zero, q)
    acc_ref[...] = jnp.zeros_like(acc_ref)
    carry_ref[...] = jnp.zeros_like(carry_ref)

    def sub(s):
        return slice(s * rows, (s + 1) * rows)

    def unit(s, j, diagonal=False, check_valid=False):
        start = pl.multiple_of(jnp.maximum(j, 0) * t, t)
        k = k_ref[0, pl.ds(start, t), :]
        v = v_ref[0, pl.ds(start, t), :]
        z = lax.dot_general(qs_ref[sub(s), :], k, (((1,), (1,)), ((), ())),
                            preferred_element_type=F32)
        if diagonal:
            causal = (lax.broadcasted_iota(jnp.int32, (rows, t), 1)
                      < (lax.broadcasted_iota(jnp.int32, (rows, t), 0) & (t - 1)))
            z = jnp.where(causal, z, NEG_BIG)
        if check_valid:
            z = jnp.where(j >= 0, z, NEG_BIG)
        sp = jnp.maximum(z, jnp.log(1.0 + jnp.exp2(jnp.minimum(z, SB_SOFTPLUS_CLAMP))) * LOG2E)
        s_excl = jnp.dot(sp.astype(BF16), u_ref[...], preferred_element_type=F32)
        carry = carry_ref[sub(s), :]
        x = (z - sp) - (s_excl + jnp.concatenate([carry] * (t // LANES), axis=1))
        carry_ref[sub(s), :] = carry + jnp.broadcast_to(
            jnp.sum(sp, axis=-1, keepdims=True), (rows, LANES))
        acc_ref[sub(s), :] += jnp.dot(jnp.exp2(x).astype(BF16), v, preferred_element_type=F32)

    base = n_sub * i
    for w in range(SB_STRAIGHT_WAVES):
        for s in range(n_sub):
            unit(s, base + s - w, diagonal=(w == 0), check_valid=(s < w))

    def sweep_rest(group):
        def wave_needed(w):
            need = jnp.bool_(False)
            for s in group:
                unsaturated = jnp.min(carry_ref[sub(s), :]) < SB_SATURATED
                need = jnp.logical_or(need, jnp.logical_and(base + s - w >= 0, unsaturated))
            return need

        def wave(state):
            w, _ = state
            for s in group:
                unit(s, base + s - w, check_valid=True)
            return w + 1, wave_needed(w + 1)

        lax.while_loop(lambda state: state[1], wave,
                       (jnp.int32(SB_STRAIGHT_WAVES), wave_needed(SB_STRAIGHT_WAVES)))

    for first_sub in range(0, n_sub, SB_EXIT_GROUP):
        sweep_rest(range(first_sub, first_sub + SB_EXIT_GROUP))
    for s in range(n_sub):
        o_ref[0, s * t:(s + 1) * t, :] = jnp.where(
            first, acc_ref[s * rows:s * rows + t, :], acc_ref[s * rows + t:(s + 1) * rows, :]).astype(BF16)


def _stick_breaking(p, u2, d_model):
    b, s, _ = p.shape
    tq = SB_Q_SUBS * SB_SUB
    pairs = SB_HEADS * SB_DIM // LANES
    base = 6 * d_model // LANES
    return pl.pallas_call(
        _sb_kernel,
        grid=(b, pairs, s // tq),
        in_specs=[
            pl.BlockSpec((1, tq, LANES), lambda bi, hp, i: (bi, i, base + hp)),
            pl.BlockSpec((1, s, LANES), lambda bi, hp, i: (bi, 0, base + pairs + hp)),
            pl.BlockSpec((1, s, LANES), lambda bi, hp, i: (bi, 0, base + 2 * pairs + hp)),
            pl.BlockSpec(u2.shape, lambda bi, hp, i: (0, 0)),
        ],
        out_specs=pl.BlockSpec((1, tq, LANES), lambda bi, hp, i: (bi, i, hp)),
        out_shape=jax.ShapeDtypeStruct((b, s, SB_HEADS * SB_DIM), BF16),
        scratch_shapes=[
            pltpu.VMEM((2 * tq, LANES), BF16),
            pltpu.VMEM((2 * tq, LANES), F32),
            pltpu.VMEM((2 * tq, LANES), F32),
        ],
        compiler_params=_params(("parallel", "parallel", "parallel"), "stick_breaking"),
        name="stick_breaking",
    )(p, p, p, u2)


def _mixout_kernel(yr_ref, ys_ref, ga_ref, gb_ref, x_ref, mod_ref, wr_ref, ws_ref, wm_ref, o_ref):
    ya = jnp.dot(yr_ref[0], wr_ref[...], preferred_element_type=F32)
    yb = jnp.dot(ys_ref[0], ws_ref[...], preferred_element_type=F32)
    merged = (jax.nn.sigmoid(ga_ref[0].astype(F32)) * ya
              + jax.nn.sigmoid(gb_ref[0].astype(F32)) * yb)
    out = jnp.dot(merged.astype(BF16), wm_ref[...], preferred_element_type=F32)
    o_ref[0] = x_ref[0] + mod_ref[0, 2:3, :] * out


def _mix_out(y_ret, y_sb, p, x, mod, w_ret_out, w_sb_out, w_mix_out):
    b, s, d = x.shape
    tm = MIX_OUT_ROWS
    ga_blk = (p.shape[2] - 2 * d) // d
    const = lambda bi, i: (0, 0)
    return pl.pallas_call(
        _mixout_kernel,
        grid=(b, s // tm),
        in_specs=[
            pl.BlockSpec((1, tm, y_ret.shape[2]), lambda bi, i: (bi, i, 0)),
            pl.BlockSpec((1, tm, y_sb.shape[2]), lambda bi, i: (bi, i, 0)),
            pl.BlockSpec((1, tm, d), lambda bi, i: (bi, i, ga_blk)),
            pl.BlockSpec((1, tm, d), lambda bi, i: (bi, i, ga_blk + 1)),
            pl.BlockSpec((1, tm, d), lambda bi, i: (bi, i, 0)),
            pl.BlockSpec((1, N_MOD, d), lambda bi, i: (bi, 0, 0)),
            pl.BlockSpec(w_ret_out.shape, const, pipeline_mode=pl.Buffered(1)),
            pl.BlockSpec(w_sb_out.shape, const, pipeline_mode=pl.Buffered(1)),
            pl.BlockSpec(w_mix_out.shape, const, pipeline_mode=pl.Buffered(1)),
        ],
        out_specs=pl.BlockSpec((1, tm, d), lambda bi, i: (bi, i, 0)),
        out_shape=jax.ShapeDtypeStruct((b, s, d), F32),
        compiler_params=_params(("parallel", "parallel"), "mix_out"),
        name="mix_out",
    )(y_ret, y_sb, p, p, x, mod, w_ret_out, w_sb_out, w_mix_out)


def _mlp_kernel(x_ref, g_ref, mod_ref, wu_ref, wd_ref, fg_ref, o_ref, acc_ref, *, final_norm, tf):
    x = x_ref[0]
    h = _norm_modulate(x, g_ref[...], mod_ref[0, 3:4, :], mod_ref[0, 4:5, :]).astype(BF16)
    for f in range(wu_ref.shape[1] // tf):
        up = jnp.dot(h, wu_ref[:, f * tf:(f + 1) * tf], preferred_element_type=F32)
        u = jnp.square(jnp.maximum(up, 0.0)).astype(BF16)
        part = jnp.dot(u, wd_ref[f * tf:(f + 1) * tf, :], preferred_element_type=F32)
        if f == 0:
            acc_ref[...] = part
        else:
            acc_ref[...] += part
    y = x + mod_ref[0, 5:6, :] * acc_ref[...]
    if final_norm:
        ms = jnp.mean(y * y, axis=-1, keepdims=True)
        y = y * lax.rsqrt(ms + EPS) * fg_ref[...]
    o_ref[0] = y


def _mlp(x, g, mod, w_up, w_down, final_g, final_norm):
    b, s, d = x.shape
    ff = w_up.shape[1]
    tm, tf = MLP_ROWS, MLP_FF_CHUNK
    const = lambda bi, i: (0, 0)
    return pl.pallas_call(
        functools.partial(_mlp_kernel, final_norm=final_norm, tf=tf),
        grid=(b, s // tm),
        in_specs=[
            pl.BlockSpec((1, tm, d), lambda bi, i: (bi, i, 0)),
            pl.BlockSpec((1, d), const),
            pl.BlockSpec((1, N_MOD, d), lambda bi, i: (bi, 0, 0)),
            pl.BlockSpec((d, ff), const, pipeline_mode=pl.Buffered(1)),
            pl.BlockSpec((ff, d), const, pipeline_mode=pl.Buffered(1)),
            pl.BlockSpec((1, d), const),
        ],
        out_specs=pl.BlockSpec((1, tm, d), lambda bi, i: (bi, i, 0)),
        out_shape=jax.ShapeDtypeStruct((b, s, d), F32),
        scratch_shapes=[pltpu.VMEM((tm, d), F32)],
        compiler_params=_params(("parallel", "parallel"), "mlp"),
        name="mlp",
    )(x, g.reshape(1, d), mod, w_up, w_down, final_g.reshape(1, d))


def _rotary_tables(s, dk):
    half = dk // 2
    inv_freq = jnp.power(ROPE_BASE, -jnp.arange(half, dtype=F32) / half)
    ang = jnp.arange(s, dtype=F32)[:, None] * inv_freq[None, :]
    return jnp.cos(ang), jnp.sin(ang)


def _decay_table():
    log_gamma = jnp.log1p(-jnp.power(2.0, -5.0 - jnp.arange(RET_HEADS, dtype=F32)))
    return jnp.stack([log_gamma, jnp.exp(RET_CHUNK * log_gamma)])


def _projection_column_scale(d_model, n):
    col = jnp.arange(n)
    is_sb_q = (col >= 6 * d_model) & (col < 6 * d_model + SB_HEADS * SB_DIM)
    return jnp.where(is_sb_q, LOG2E * SB_DIM ** -0.5, 1.0).astype(F32).reshape(1, n)


def kernel(x, c, norm_mix_g, w_in, w_ret_out, w_sb_out, w_mix_out, norm_mlp_g, w_up, w_down,
           w_ada, b_ada, final_g):
    depth = w_in.shape[0]
    b, s, d = x.shape
    cos, sin = _rotary_tables(s, d // RET_HEADS)
    lg_tab = _decay_table()
    col_scale = _projection_column_scale(d, w_in.shape[2])
    t = SB_SUB
    u = (lax.broadcasted_iota(jnp.int32, (t, t), 0)
         > lax.broadcasted_iota(jnp.int32, (t, t), 1)).astype(BF16)

    mod_all = _ada_modulation(c, w_ada, b_ada).reshape(depth, b, N_MOD, d)
    for l in range(depth):
        mod = mod_all[l]
        p = _in_projection(x, norm_mix_g[l], mod, w_in[l].astype(BF16), col_scale)
        y_ret = _retention(p, lg_tab, cos, sin, d)
        y_sb = _stick_breaking(p, u, d)
        x = _mix_out(y_ret, y_sb, p, x, mod, w_ret_out[l].astype(BF16),
                     w_sb_out[l].astype(BF16), w_mix_out[l].astype(BF16))
        x = _mlp(x, norm_mlp_g[l], mod, w_up[l].astype(BF16), w_down[l].astype(BF16),
                 final_g, final_norm=(l == depth - 1))
    return x
```

```python
import functools

import jax
import jax.numpy as jnp
from jax import lax
from jax.experimental import pallas as pl
from jax.experimental.pallas import tpu as pltpu

F32 = jnp.float32
BF16 = jnp.bfloat16

RET_HEADS = 4
SB_HEADS = 16
SB_DIM = 64
N_MOD = 6
ROPE_BASE = 10000.0
EPS = 1e-6
GN_EPS = 1e-5

RET_CHUNK = 256
SB_SUB = 256
SB_Q_SUBS = 8
SB_EXIT_GROUP = 2
SB_STRAIGHT_WAVES = 3
NEG_BIG = -1e30
SB_SATURATED = 152.0
SB_SOFTPLUS_CLAMP = 64.0
LOG2E = 1.4426950408889634
LANES = 128
MIB = 1024 * 1024

ADA_COL_TILES = 4
IN_PROJ_ROWS = 1024
IN_PROJ_COL_TILES = 4
MIX_OUT_ROWS = 512
MLP_ROWS = 1024
MLP_FF_CHUNK = 1024
VMEM_LIMIT_MIB = {"ada": 32, "in_proj": 56, "retention": 32, "stick_breaking": 32,
                  "mix_out": 40, "mlp": 56}


def _params(semantics, call):
    return pltpu.CompilerParams(dimension_semantics=semantics,
                                vmem_limit_bytes=VMEM_LIMIT_MIB[call] * MIB)


def _ada_kernel(c_ref, w_ref, b_ref, o_ref):
    c = c_ref[...]
    c_act = c * jax.nn.sigmoid(c)
    o_ref[0] = jnp.dot(c_act, w_ref[0], preferred_element_type=F32) + b_ref[0]


def _ada_modulation(c, w_ada, b_ada):
    depth, d, n = w_ada.shape
    b = c.shape[0]
    tn = n // ADA_COL_TILES
    return pl.pallas_call(
        _ada_kernel,
        grid=(depth, n // tn),
        in_specs=[
            pl.BlockSpec((b, d), lambda l, j: (0, 0)),
            pl.BlockSpec((1, d, tn), lambda l, j: (l, 0, j)),
            pl.BlockSpec((1, 1, tn), lambda l, j: (l, 0, j)),
        ],
        out_specs=pl.BlockSpec((1, b, tn), lambda l, j: (l, 0, j)),
        out_shape=jax.ShapeDtypeStruct((depth, b, n), F32),
        compiler_params=_params(("parallel", "parallel"), "ada"),
        name="ada_modulation",
    )(c, w_ada, b_ada.reshape(depth, 1, n))


def _norm_modulate(x, g, shift, scale):
    ms = jnp.mean(x * x, axis=-1, keepdims=True)
    y = x * lax.rsqrt(ms + EPS) * g
    return y * (1.0 + scale) + shift


def _inproj_kernel(x_ref, g_ref, mod_ref, w_ref, cs_ref, o_ref, h_ref):
    @pl.when(pl.program_id(2) == 0)
    def _():
        h = _norm_modulate(x_ref[0], g_ref[...], mod_ref[0, 0:1, :], mod_ref[0, 1:2, :])
        h_ref[...] = h.astype(BF16)

    o_ref[0] = (jnp.dot(h_ref[...], w_ref[...], preferred_element_type=F32)
                * cs_ref[...]).astype(BF16)


def _in_projection(x, g, mod, w_in, col_scale):
    b, s, d = x.shape
    n = w_in.shape[1]
    tm, tn = IN_PROJ_ROWS, n // IN_PROJ_COL_TILES
    return pl.pallas_call(
        _inproj_kernel,
        grid=(b, s // tm, n // tn),
        in_specs=[
            pl.BlockSpec((1, tm, d), lambda bi, i, j: (bi, i, 0)),
            pl.BlockSpec((1, d), lambda bi, i, j: (0, 0)),
            pl.BlockSpec((1, N_MOD, d), lambda bi, i, j: (bi, 0, 0)),
            pl.BlockSpec((d, tn), lambda bi, i, j: (0, j)),
            pl.BlockSpec((1, tn), lambda bi, i, j: (0, j)),
        ],
        out_specs=pl.BlockSpec((1, tm, tn), lambda bi, i, j: (bi, i, j)),
        out_shape=jax.ShapeDtypeStruct((b, s, n), BF16),
        scratch_shapes=[pltpu.VMEM((tm, d), BF16)],
        compiler_params=_params(("parallel", "parallel", "arbitrary"), "in_proj"),
        name="in_projection",
    )(x, g.reshape(1, d), mod, w_in, col_scale)


def _retention_kernel(lg_ref, q_ref, k_ref, v_ref, rg_ref, cos_ref, sin_ref, o_ref,
                      r_ref, dec_ref, xi_ref, zeta_ref):
    c = q_ref.shape[1]
    n_heads = r_ref.shape[0]
    dk, dv = r_ref.shape[1], r_ref.shape[2]
    half = dk // 2

    @pl.when(pl.program_id(1) == 0)
    def _():
        r_ref[...] = jnp.zeros_like(r_ref)
        rel = (lax.broadcasted_iota(jnp.int32, (c, c), 0)
               - lax.broadcasted_iota(jnp.int32, (c, c), 1)).astype(F32)
        idx = lax.broadcasted_iota(jnp.int32, (c, 1), 0).astype(F32)
        for hd in range(n_heads):
            lg = lg_ref[0, hd]
            dec_ref[hd] = jnp.where(rel >= 0, jnp.exp(jnp.maximum(rel, 0.0) * lg), 0.0)
            xi_ref[hd] = jnp.exp((idx + 1.0) * lg)
            zeta_ref[hd] = jnp.exp((c - 1.0 - idx) * lg)

    cos = cos_ref[...]
    sin = sin_ref[...]

    def rot(t):
        t1, t2 = t[:, :half], t[:, half:]
        return jnp.concatenate([t1 * cos - t2 * sin, t1 * sin + t2 * cos], axis=-1)

    for hd in range(n_heads):
        gamma_c = lg_ref[1, hd]
        qr = rot(q_ref[0, :, hd * dk:(hd + 1) * dk].astype(F32)).astype(BF16)
        kr = rot(k_ref[0, :, hd * dk:(hd + 1) * dk].astype(F32)) * (dk ** -0.5)
        v = v_ref[0, :, hd * dv:(hd + 1) * dv]
        scores = lax.dot_general(qr, kr.astype(BF16), (((1,), (1,)), ((), ())),
                                 preferred_element_type=F32) * dec_ref[hd]
        r_old = r_ref[hd]
        o = jnp.dot(scores.astype(BF16), v, preferred_element_type=F32)
        o = o + jnp.dot(qr, r_old.astype(BF16), preferred_element_type=F32) * xi_ref[hd]
        kz = (kr * zeta_ref[hd]).astype(BF16)
        r_ref[hd] = r_old * gamma_c + lax.dot_general(kz, v, (((0,), (0,)), ((), ())),
                                                      preferred_element_type=F32)
        mu = jnp.mean(o, axis=-1, keepdims=True)
        oc = o - mu
        var = jnp.mean(oc * oc, axis=-1, keepdims=True)
        yn = oc * lax.rsqrt(var + GN_EPS)
        rg = rg_ref[0, :, hd * dv:(hd + 1) * dv].astype(F32)
        o_ref[0, :, hd * dv:(hd + 1) * dv] = (rg * jax.nn.sigmoid(rg) * yn).astype(BF16)


def _retention(p, lg_tab, cos, sin, d_model):
    b, s, _ = p.shape
    dk = d_model // RET_HEADS
    dv = 2 * dk
    qk_w, v_w = RET_HEADS * dk, RET_HEADS * dv
    c = RET_CHUNK
    kq, kk, kv, kg = 0, 1, (2 * qk_w) // v_w, (2 * qk_w) // v_w + 1
    return pl.pallas_call(
        _retention_kernel,
        grid=(b, s // c),
        in_specs=[
            pl.BlockSpec(memory_space=pltpu.SMEM),
            pl.BlockSpec((1, c, qk_w), lambda bi, n: (bi, n, kq)),
            pl.BlockSpec((1, c, qk_w), lambda bi, n: (bi, n, kk)),
            pl.BlockSpec((1, c, v_w), lambda bi, n: (bi, n, kv)),
            pl.BlockSpec((1, c, v_w), lambda bi, n: (bi, n, kg)),
            pl.BlockSpec((c, dk // 2), lambda bi, n: (n, 0)),
            pl.BlockSpec((c, dk // 2), lambda bi, n: (n, 0)),
        ],
        out_specs=pl.BlockSpec((1, c, v_w), lambda bi, n: (bi, n, 0)),
        out_shape=jax.ShapeDtypeStruct((b, s, v_w), BF16),
        scratch_shapes=[
            pltpu.VMEM((RET_HEADS, dk, dv), F32),
            pltpu.VMEM((RET_HEADS, c, c), F32),
            pltpu.VMEM((RET_HEADS, c, 1), F32),
            pltpu.VMEM((RET_HEADS, c, 1), F32),
        ],
        compiler_params=_params(("parallel", "arbitrary"), "retention"),
        name="retention",
    )(lg_tab, p, p, p, p, cos, sin)


def _sb_kernel(q_ref, k_ref, v_ref, u_ref, o_ref, qs_ref, acc_ref, carry_ref):
    t = SB_SUB
    n_sub = SB_Q_SUBS
    rows = 2 * t
    i = pl.program_id(2)
    lane = lax.broadcasted_iota(jnp.int32, (1, LANES), 1)
    first = lane < SB_DIM
    for s in range(n_sub):
        q = q_ref[0, s * t:(s + 1) * t, :]
        zero = jnp.zeros_like(q)
        qs_ref[s * rows:s * rows + t, :] = jnp.where(first, q, zero)
        qs_ref[s * rows + t:(s + 1) * rows, :] = jnp.where(first, zero, q)
    acc_ref[...] = jnp.zeros_like(acc_ref)
    carry_ref[...] = jnp.zeros_like(carry_ref)

    def sub(s):
        return slice(s * rows, (s + 1) * rows)

    def unit(s, j, diagonal=False, check_valid=False):
        start = pl.multiple_of(jnp.maximum(j, 0) * t, t)
        k = k_ref[0, pl.ds(start, t), :]
        v = v_ref[0, pl.ds(start, t), :]
        z = lax.dot_general(qs_ref[sub(s), :], k, (((1,), (1,)), ((), ())),
                            preferred_element_type=F32)
        if diagonal:
            causal = (lax.broadcasted_iota(jnp.int32, (rows, t), 1)
                      < (lax.broadcasted_iota(jnp.int32, (rows, t), 0) & (t - 1)))
            z = jnp.where(causal, z, NEG_BIG)
        sp = jnp.maximum(z, jnp.log(1.0 + jnp.exp2(jnp.minimum(z, SB_SOFTPLUS_CLAMP))) * LOG2E)
        s_excl = jnp.dot(sp.astype(BF16), u_ref[...], preferred_element_type=F32)
        carry = carry_ref[sub(s), :]
        spent = carry - jnp.where(j >= 0, 0.0, NEG_BIG) if check_valid else carry
        x = (z - sp) - (s_excl + jnp.concatenate([spent] * (t // LANES), axis=1))
        carry_ref[sub(s), :] = carry + jnp.broadcast_to(
            jnp.sum(sp, axis=-1, keepdims=True), (rows, LANES))
        acc_ref[sub(s), :] += jnp.dot(jnp.exp2(x).astype(BF16), v, preferred_element_type=F32)

    base = n_sub * i
    for w in range(SB_STRAIGHT_WAVES):
        for s in range(n_sub):
            unit(s, base + s - w, diagonal=(w == 0), check_valid=(s < w))

    def sweep_rest(group):
        def wave_needed(w):
            need = jnp.bool_(False)
            for s in group:
                unsaturated = jnp.min(carry_ref[sub(s), :]) < SB_SATURATED
                need = jnp.logical_or(need, jnp.logical_and(base + s - w >= 0, unsaturated))
            return need

        def wave(state):
            w, _ = state
            for s in group:
                unit(s, base + s - w, check_valid=True)
            return w + 1, wave_needed(w + 1)

        lax.while_loop(lambda state: state[1], wave,
                       (jnp.int32(SB_STRAIGHT_WAVES), wave_needed(SB_STRAIGHT_WAVES)))

    for first_sub in range(0, n_sub, SB_EXIT_GROUP):
        sweep_rest(range(first_sub, first_sub + SB_EXIT_GROUP))
    for s in range(n_sub):
        o_ref[0, s * t:(s + 1) * t, :] = jnp.where(
            first, acc_ref[s * rows:s * rows + t, :], acc_ref[s * rows + t:(s + 1) * rows, :]).astype(BF16)


def _stick_breaking(p, u2, d_model):
    b, s, _ = p.shape
    tq = SB_Q_SUBS * SB_SUB
    pairs = SB_HEADS * SB_DIM // LANES
    base = 6 * d_model // LANES
    return pl.pallas_call(
        _sb_kernel,
        grid=(b, pairs, s // tq),
        in_specs=[
            pl.BlockSpec((1, tq, LANES), lambda bi, hp, i: (bi, i, base + hp)),
            pl.BlockSpec((1, s, LANES), lambda bi, hp, i: (bi, 0, base + pairs + hp)),
            pl.BlockSpec((1, s, LANES), lambda bi, hp, i: (bi, 0, base + 2 * pairs + hp)),
            pl.BlockSpec(u2.shape, lambda bi, hp, i: (0, 0)),
        ],
        out_specs=pl.BlockSpec((1, tq, LANES), lambda bi, hp, i: (bi, i, hp)),
        out_shape=jax.ShapeDtypeStruct((b, s, SB_HEADS * SB_DIM), BF16),
        scratch_shapes=[
            pltpu.VMEM((2 * tq, LANES), BF16),
            pltpu.VMEM((2 * tq, LANES), F32),
            pltpu.VMEM((2 * tq, LANES), F32),
        ],
        compiler_params=_params(("parallel", "parallel", "parallel"), "stick_breaking"),
        name="stick_breaking",
    )(p, p, p, u2)


def _mixout_kernel(yr_ref, ys_ref, ga_ref, gb_ref, x_ref, mod_ref, wr_ref, ws_ref, wm_ref, o_ref):
    ya = jnp.dot(yr_ref[0], wr_ref[...], preferred_element_type=F32)
    yb = jnp.dot(ys_ref[0], ws_ref[...], preferred_element_type=F32)
    merged = (jax.nn.sigmoid(ga_ref[0].astype(F32)) * ya
              + jax.nn.sigmoid(gb_ref[0].astype(F32)) * yb)
    out = jnp.dot(merged.astype(BF16), wm_ref[...], preferred_element_type=F32)
    o_ref[0] = x_ref[0] + mod_ref[0, 2:3, :] * out


def _mix_out(y_ret, y_sb, p, x, mod, w_ret_out, w_sb_out, w_mix_out):
    b, s, d = x.shape
    tm = MIX_OUT_ROWS
    ga_blk = (p.shape[2] - 2 * d) // d
    const = lambda bi, i: (0, 0)
    return pl.pallas_call(
        _mixout_kernel,
        grid=(b, s // tm),
        in_specs=[
            pl.BlockSpec((1, tm, y_ret.shape[2]), lambda bi, i: (bi, i, 0)),
            pl.BlockSpec((1, tm, y_sb.shape[2]), lambda bi, i: (bi, i, 0)),
            pl.BlockSpec((1, tm, d), lambda bi, i: (bi, i, ga_blk)),
            pl.BlockSpec((1, tm, d), lambda bi, i: (bi, i, ga_blk + 1)),
            pl.BlockSpec((1, tm, d), lambda bi, i: (bi, i, 0)),
            pl.BlockSpec((1, N_MOD, d), lambda bi, i: (bi, 0, 0)),
            pl.BlockSpec(w_ret_out.shape, const, pipeline_mode=pl.Buffered(1)),
            pl.BlockSpec(w_sb_out.shape, const, pipeline_mode=pl.Buffered(1)),
            pl.BlockSpec(w_mix_out.shape, const, pipeline_mode=pl.Buffered(1)),
        ],
        out_specs=pl.BlockSpec((1, tm, d), lambda bi, i: (bi, i, 0)),
        out_shape=jax.ShapeDtypeStruct((b, s, d), F32),
        compiler_params=_params(("parallel", "parallel"), "mix_out"),
        name="mix_out",
    )(y_ret, y_sb, p, p, x, mod, w_ret_out, w_sb_out, w_mix_out)


def _mlp_kernel(x_ref, g_ref, mod_ref, wu_ref, wd_ref, fg_ref, o_ref, acc_ref, *, final_norm, tf):
    x = x_ref[0]
    h = _norm_modulate(x, g_ref[...], mod_ref[0, 3:4, :], mod_ref[0, 4:5, :]).astype(BF16)
    for f in range(wu_ref.shape[1] // tf):
        up = jnp.dot(h, wu_ref[:, f * tf:(f + 1) * tf], preferred_element_type=F32)
        u = jnp.square(jnp.maximum(up, 0.0)).astype(BF16)
        part = jnp.dot(u, wd_ref[f * tf:(f + 1) * tf, :], preferred_element_type=F32)
        if f == 0:
            acc_ref[...] = part
        else:
            acc_ref[...] += part
    y = x + mod_ref[0, 5:6, :] * acc_ref[...]
    if final_norm:
        ms = jnp.mean(y * y, axis=-1, keepdims=True)
        y = y * lax.rsqrt(ms + EPS) * fg_ref[...]
    o_ref[0] = y


def _mlp(x, g, mod, w_up, w_down, final_g, final_norm):
    b, s, d = x.shape
    ff = w_up.shape[1]
    tm, tf = MLP_ROWS, MLP_FF_CHUNK
    const = lambda bi, i: (0, 0)
    return pl.pallas_call(
        functools.partial(_mlp_kernel, final_norm=final_norm, tf=tf),
        grid=(b, s // tm),
        in_specs=[
            pl.BlockSpec((1, tm, d), lambda bi, i: (bi, i, 0)),
            pl.BlockSpec((1, d), const),
            pl.BlockSpec((1, N_MOD, d), lambda bi, i: (bi, 0, 0)),
            pl.BlockSpec((d, ff), const, pipeline_mode=pl.Buffered(1)),
            pl.BlockSpec((ff, d), const, pipeline_mode=pl.Buffered(1)),
            pl.BlockSpec((1, d), const),
        ],
        out_specs=pl.BlockSpec((1, tm, d), lambda bi, i: (bi, i, 0)),
        out_shape=jax.ShapeDtypeStruct((b, s, d), F32),
        scratch_shapes=[pltpu.VMEM((tm, d), F32)],
        compiler_params=_params(("parallel", "parallel"), "mlp"),
        name="mlp",
    )(x, g.reshape(1, d), mod, w_up, w_down, final_g.reshape(1, d))


def _rotary_tables(s, dk):
    half = dk // 2
    inv_freq = jnp.power(ROPE_BASE, -jnp.arange(half, dtype=F32) / half)
    ang = jnp.arange(s, dtype=F32)[:, None] * inv_freq[None, :]
    return jnp.cos(ang), jnp.sin(ang)


def _decay_table():
    log_gamma = jnp.log1p(-jnp.power(2.0, -5.0 - jnp.arange(RET_HEADS, dtype=F32)))
    return jnp.stack([log_gamma, jnp.exp(RET_CHUNK * log_gamma)])


def _projection_column_scale(d_model, n):
    col = jnp.arange(n)
    is_sb_q = (col >= 6 * d_model) & (col < 6 * d_model + SB_HEADS * SB_DIM)
    return jnp.where(is_sb_q, LOG2E * SB_DIM ** -0.5, 1.0).astype(F32).reshape(1, n)


def kernel(x, c, norm_mix_g, w_in, w_ret_out, w_sb_out, w_mix_out, norm_mlp_g, w_up, w_down,
           w_ada, b_ada, final_g):
    depth = w_in.shape[0]
    b, s, d = x.shape
    cos, sin = _rotary_tables(s, d // RET_HEADS)
    lg_tab = _decay_table()
    col_scale = _projection_column_scale(d, w_in.shape[2])
    t = SB_SUB
    u = (lax.broadcasted_iota(jnp.int32, (t, t), 0)
         > lax.broadcasted_iota(jnp.int32, (t, t), 1)).astype(BF16)

    mod_all = _ada_modulation(c, w_ada, b_ada).reshape(depth, b, N_MOD, d)
    for l in range(depth):
        mod = mod_all[l]
        p = _in_projection(x, norm_mix_g[l], mod, w_in[l].astype(BF16), col_scale)
        y_ret = _retention(p, lg_tab, cos, sin, d)
        y_sb = _stick_breaking(p, u, d)
        x = _mix_out(y_ret, y_sb, p, x, mod, w_ret_out[l].astype(BF16),
                     w_sb_out[l].astype(BF16), w_mix_out[l].astype(BF16))
        x = _mlp(x, norm_mlp_g[l], mod, w_up[l].astype(BF16), w_down[l].astype(BF16),
                 final_g, final_norm=(l == depth - 1))
    return x
```

```python
import functools

import jax
import jax.numpy as jnp
import numpy as np
from jax import lax
from jax.experimental import pallas as pl
from jax.experimental.pallas import tpu as pltpu

F32 = jnp.float32
BF16 = jnp.bfloat16

RET_HEADS = 4
SB_HEADS = 16
SB_DIM = 64
N_MOD = 6
ROPE_BASE = 10000.0
EPS = 1e-6
GN_EPS = 1e-5

RET_CHUNK = 256
SB_SUB = 256
SB_Q_SUBS = 8
SB_EXIT_GROUP = 2
SB_STRAIGHT_WAVES = 3
NEG_BIG = -1e30
SB_SATURATED = 152.0
SB_SOFTPLUS_CLAMP = 64.0
LOG2E = 1.4426950408889634
LANES = 128
MIB = 1024 * 1024

ADA_COL_TILES = 4
IN_PROJ_ROWS = 1024
IN_PROJ_COL_TILES = 4
MIX_OUT_ROWS = 512
MLP_ROWS = 1024
MLP_FF_CHUNK = 1024
VMEM_LIMIT_MIB = {"ada": 32, "in_proj": 56, "retention": 32, "stick_breaking": 32,
                  "mix_out": 40, "mlp": 56}


def _params(semantics, call):
    return pltpu.CompilerParams(dimension_semantics=semantics,
                                vmem_limit_bytes=VMEM_LIMIT_MIB[call] * MIB)


def _ada_kernel(c_ref, w_ref, b_ref, o_ref):
    c = c_ref[...]
    c_act = c * jax.nn.sigmoid(c)
    o_ref[0] = jnp.dot(c_act, w_ref[0], preferred_element_type=F32) + b_ref[0]


def _ada_modulation(c, w_ada, b_ada):
    depth, d, n = w_ada.shape
    b = c.shape[0]
    tn = n // ADA_COL_TILES
    return pl.pallas_call(
        _ada_kernel,
        grid=(depth, n // tn),
        in_specs=[
            pl.BlockSpec((b, d), lambda l, j: (0, 0)),
            pl.BlockSpec((1, d, tn), lambda l, j: (l, 0, j)),
            pl.BlockSpec((1, 1, tn), lambda l, j: (l, 0, j)),
        ],
        out_specs=pl.BlockSpec((1, b, tn), lambda l, j: (l, 0, j)),
        out_shape=jax.ShapeDtypeStruct((depth, b, n), F32),
        compiler_params=_params(("parallel", "parallel"), "ada"),
        name="ada_modulation",
    )(c, w_ada, b_ada.reshape(depth, 1, n))


def _norm_modulate(x, g, shift, scale):
    ms = jnp.mean(x * x, axis=-1, keepdims=True)
    y = x * lax.rsqrt(ms + EPS) * g
    return y * (1.0 + scale) + shift


def _inproj_kernel(x_ref, g_ref, mod_ref, w_ref, cs_ref, o_ref, h_ref):
    @pl.when(pl.program_id(2) == 0)
    def _():
        h = _norm_modulate(x_ref[0], g_ref[...], mod_ref[0, 0:1, :], mod_ref[0, 1:2, :])
        h_ref[...] = h.astype(BF16)

    o_ref[0] = (jnp.dot(h_ref[...], w_ref[...], preferred_element_type=F32)
                * cs_ref[...]).astype(BF16)


def _in_projection(x, g, mod, w_in, col_scale):
    b, s, d = x.shape
    n = w_in.shape[1]
    tm, tn = IN_PROJ_ROWS, n // IN_PROJ_COL_TILES
    return pl.pallas_call(
        _inproj_kernel,
        grid=(b, s // tm, n // tn),
        in_specs=[
            pl.BlockSpec((1, tm, d), lambda bi, i, j: (bi, i, 0)),
            pl.BlockSpec((1, d), lambda bi, i, j: (0, 0)),
            pl.BlockSpec((1, N_MOD, d), lambda bi, i, j: (bi, 0, 0)),
            pl.BlockSpec((d, tn), lambda bi, i, j: (0, j)),
            pl.BlockSpec((1, tn), lambda bi, i, j: (0, j)),
        ],
        out_specs=pl.BlockSpec((1, tm, tn), lambda bi, i, j: (bi, i, j)),
        out_shape=jax.ShapeDtypeStruct((b, s, n), BF16),
        scratch_shapes=[pltpu.VMEM((tm, d), BF16)],
        compiler_params=_params(("parallel", "parallel", "arbitrary"), "in_proj"),
        name="in_projection",
    )(x, g.reshape(1, d), mod, w_in, col_scale)


def _retention_kernel(lg_ref, q_ref, k_ref, v_ref, rg_ref, cos_ref, sin_ref, o_ref,
                      r_ref, dec_ref, xi_ref, zeta_ref):
    c = q_ref.shape[1]
    n_heads = r_ref.shape[0]
    dk, dv = r_ref.shape[1], r_ref.shape[2]
    half = dk // 2

    @pl.when(pl.program_id(1) == 0)
    def _():
        r_ref[...] = jnp.zeros_like(r_ref)
        rel = (lax.broadcasted_iota(jnp.int32, (c, c), 0)
               - lax.broadcasted_iota(jnp.int32, (c, c), 1)).astype(F32)
        idx = lax.broadcasted_iota(jnp.int32, (c, 1), 0).astype(F32)
        for hd in range(n_heads):
            lg = lg_ref[0, hd]
            dec_ref[hd] = jnp.where(rel >= 0, jnp.exp(jnp.maximum(rel, 0.0) * lg), 0.0)
            xi_ref[hd] = jnp.exp((idx + 1.0) * lg)
            zeta_ref[hd] = jnp.exp((c - 1.0 - idx) * lg)

    cos = cos_ref[...]
    sin = sin_ref[...]

    def rot(t):
        t1, t2 = t[:, :half], t[:, half:]
        return jnp.concatenate([t1 * cos - t2 * sin, t1 * sin + t2 * cos], axis=-1)

    for hd in range(n_heads):
        gamma_c = lg_ref[1, hd]
        qr = rot(q_ref[0, :, hd * dk:(hd + 1) * dk].astype(F32)).astype(BF16)
        kr = rot(k_ref[0, :, hd * dk:(hd + 1) * dk].astype(F32)) * (dk ** -0.5)
        v = v_ref[0, :, hd * dv:(hd + 1) * dv]
        scores = lax.dot_general(qr, kr.astype(BF16), (((1,), (1,)), ((), ())),
                                 preferred_element_type=F32) * dec_ref[hd]
        r_old = r_ref[hd]
        o = jnp.dot(scores.astype(BF16), v, preferred_element_type=F32)
        o = o + jnp.dot(qr, r_old.astype(BF16), preferred_element_type=F32) * xi_ref[hd]
        kz = (kr * zeta_ref[hd]).astype(BF16)
        r_ref[hd] = r_old * gamma_c + lax.dot_general(kz, v, (((0,), (0,)), ((), ())),
                                                      preferred_element_type=F32)
        mu = jnp.mean(o, axis=-1, keepdims=True)
        oc = o - mu
        var = jnp.mean(oc * oc, axis=-1, keepdims=True)
        yn = oc * lax.rsqrt(var + GN_EPS)
        rg = rg_ref[0, :, hd * dv:(hd + 1) * dv].astype(F32)
        o_ref[0, :, hd * dv:(hd + 1) * dv] = (rg * jax.nn.sigmoid(rg) * yn).astype(BF16)


def _retention(p, lg_tab, cos, sin, d_model):
    b, s, _ = p.shape
    dk = d_model // RET_HEADS
    dv = 2 * dk
    qk_w, v_w = RET_HEADS * dk, RET_HEADS * dv
    c = RET_CHUNK
    kq, kk, kv, kg = 0, 1, (2 * qk_w) // v_w, (2 * qk_w) // v_w + 1
    return pl.pallas_call(
        _retention_kernel,
        grid=(b, s // c),
        in_specs=[
            pl.BlockSpec(memory_space=pltpu.SMEM),
            pl.BlockSpec((1, c, qk_w), lambda bi, n: (bi, n, kq)),
            pl.BlockSpec((1, c, qk_w), lambda bi, n: (bi, n, kk)),
            pl.BlockSpec((1, c, v_w), lambda bi, n: (bi, n, kv)),
            pl.BlockSpec((1, c, v_w), lambda bi, n: (bi, n, kg)),
            pl.BlockSpec((c, dk // 2), lambda bi, n: (n, 0)),
            pl.BlockSpec((c, dk // 2), lambda bi, n: (n, 0)),
        ],
        out_specs=pl.BlockSpec((1, c, v_w), lambda bi, n: (bi, n, 0)),
        out_shape=jax.ShapeDtypeStruct((b, s, v_w), BF16),
        scratch_shapes=[
            pltpu.VMEM((RET_HEADS, dk, dv), F32),
            pltpu.VMEM((RET_HEADS, c, c), F32),
            pltpu.VMEM((RET_HEADS, c, 1), F32),
            pltpu.VMEM((RET_HEADS, c, 1), F32),
        ],
        compiler_params=_params(("parallel", "arbitrary"), "retention"),
        name="retention",
    )(lg_tab, p, p, p, p, cos, sin)


def _sb_kernel(q_ref, k_ref, v_ref, u_ref, o_ref, qs_ref, acc_ref, carry_ref):
    t = SB_SUB
    n_sub = SB_Q_SUBS
    rows = 2 * t
    i = pl.program_id(2)
    lane = lax.broadcasted_iota(jnp.int32, (1, LANES), 1)
    first = lane < SB_DIM
    for s in range(n_sub):
        q = q_ref[0, s * t:(s + 1) * t, :]
        zero = jnp.zeros_like(q)
        qs_ref[s * rows:s * rows + t, :] = jnp.where(first, q, zero)
        qs_ref[s * rows + t:(s + 1) * rows, :] = jnp.where(first, zero, q)
    acc_ref[...] = jnp.zeros_like(acc_ref)
    carry_ref[...] = jnp.zeros_like(carry_ref)

    def sub(s):
        return slice(s * rows, (s + 1) * rows)

    def unit(s, j, diagonal=False, check_valid=False):
        start = pl.multiple_of(jnp.maximum(j, 0) * t, t)
        k = k_ref[0, pl.ds(start, t), :]
        v = v_ref[0, pl.ds(start, t), :]
        z = lax.dot_general(qs_ref[sub(s), :], k, (((1,), (1,)), ((), ())),
                            preferred_element_type=F32)
        if diagonal:
            causal = (lax.broadcasted_iota(jnp.int32, (rows, t), 1)
                      < (lax.broadcasted_iota(jnp.int32, (rows, t), 0) & (t - 1)))
            z = jnp.where(causal, z, NEG_BIG)
        sp = jnp.maximum(z, jnp.log(1.0 + jnp.exp2(jnp.minimum(z, SB_SOFTPLUS_CLAMP))) * LOG2E)
        s_excl = jnp.dot(sp.astype(BF16), u_ref[...], preferred_element_type=F32)
        carry = carry_ref[sub(s), :]
        spent = carry - jnp.where(j >= 0, 0.0, NEG_BIG) if check_valid else carry
        x = (z - sp) - (s_excl + jnp.concatenate([spent] * (t // LANES), axis=1))
        carry_ref[sub(s), :] = carry + jnp.broadcast_to(
            jnp.sum(sp, axis=-1, keepdims=True), (rows, LANES))
        acc_ref[sub(s), :] += jnp.dot(jnp.exp2(x).astype(BF16), v, preferred_element_type=F32)

    base = n_sub * i
    for w in range(SB_STRAIGHT_WAVES):
        for s in range(n_sub):
            unit(s, base + s - w, diagonal=(w == 0), check_valid=(s < w))

    def sweep_rest(group):
        def wave_needed(w):
            need = jnp.bool_(False)
            for s in group:
                unsaturated = jnp.min(carry_ref[sub(s), :]) < SB_SATURATED
                need = jnp.logical_or(need, jnp.logical_and(base + s - w >= 0, unsaturated))
            return need

        def wave(state):
            w, _ = state
            for s in group:
                unit(s, base + s - w, check_valid=True)
            return w + 1, wave_needed(w + 1)

        lax.while_loop(lambda state: state[1], wave,
                       (jnp.int32(SB_STRAIGHT_WAVES), wave_needed(SB_STRAIGHT_WAVES)))

    for first_sub in range(0, n_sub, SB_EXIT_GROUP):
        sweep_rest(range(first_sub, first_sub + SB_EXIT_GROUP))
    for s in range(n_sub):
        o_ref[0, s * t:(s + 1) * t, :] = jnp.where(
            first, acc_ref[s * rows:s * rows + t, :], acc_ref[s * rows + t:(s + 1) * rows, :]).astype(BF16)


def _stick_breaking(p, u2, d_model):
    b, s, _ = p.shape
    tq = SB_Q_SUBS * SB_SUB
    pairs = SB_HEADS * SB_DIM // LANES
    base = 6 * d_model // LANES
    return pl.pallas_call(
        _sb_kernel,
        grid=(b, pairs, s // tq),
        in_specs=[
            pl.BlockSpec((1, tq, LANES), lambda bi, hp, i: (bi, i, base + hp)),
            pl.BlockSpec((1, s, LANES), lambda bi, hp, i: (bi, 0, base + pairs + hp)),
            pl.BlockSpec((1, s, LANES), lambda bi, hp, i: (bi, 0, base + 2 * pairs + hp)),
            pl.BlockSpec(u2.shape, lambda bi, hp, i: (0, 0)),
        ],
        out_specs=pl.BlockSpec((1, tq, LANES), lambda bi, hp, i: (bi, i, hp)),
        out_shape=jax.ShapeDtypeStruct((b, s, SB_HEADS * SB_DIM), BF16),
        scratch_shapes=[
            pltpu.VMEM((2 * tq, LANES), BF16),
            pltpu.VMEM((2 * tq, LANES), F32),
            pltpu.VMEM((2 * tq, LANES), F32),
        ],
        compiler_params=_params(("parallel", "parallel", "parallel"), "stick_breaking"),
        name="stick_breaking",
    )(p, p, p, u2)


def _mixout_kernel(yr_ref, ys_ref, ga_ref, gb_ref, x_ref, mod_ref, wr_ref, ws_ref, wm_ref, o_ref):
    ya = jnp.dot(yr_ref[0], wr_ref[...], preferred_element_type=F32)
    yb = jnp.dot(ys_ref[0], ws_ref[...], preferred_element_type=F32)
    merged = (jax.nn.sigmoid(ga_ref[0].astype(F32)) * ya
              + jax.nn.sigmoid(gb_ref[0].astype(F32)) * yb)
    out = jnp.dot(merged.astype(BF16), wm_ref[...], preferred_element_type=F32)
    o_ref[0] = x_ref[0] + mod_ref[0, 2:3, :] * out


def _mix_out(y_ret, y_sb, p, x, mod, w_ret_out, w_sb_out, w_mix_out):
    b, s, d = x.shape
    tm = MIX_OUT_ROWS
    ga_blk = (p.shape[2] - 2 * d) // d
    const = lambda bi, i: (0, 0)
    return pl.pallas_call(
        _mixout_kernel,
        grid=(b, s // tm),
        in_specs=[
            pl.BlockSpec((1, tm, y_ret.shape[2]), lambda bi, i: (bi, i, 0)),
            pl.BlockSpec((1, tm, y_sb.shape[2]), lambda bi, i: (bi, i, 0)),
            pl.BlockSpec((1, tm, d), lambda bi, i: (bi, i, ga_blk)),
            pl.BlockSpec((1, tm, d), lambda bi, i: (bi, i, ga_blk + 1)),
            pl.BlockSpec((1, tm, d), lambda bi, i: (bi, i, 0)),
            pl.BlockSpec((1, N_MOD, d), lambda bi, i: (bi, 0, 0)),
            pl.BlockSpec(w_ret_out.shape, const, pipeline_mode=pl.Buffered(1)),
            pl.BlockSpec(w_sb_out.shape, const, pipeline_mode=pl.Buffered(1)),
            pl.BlockSpec(w_mix_out.shape, const, pipeline_mode=pl.Buffered(1)),
        ],
        out_specs=pl.BlockSpec((1, tm, d), lambda bi, i: (bi, i, 0)),
        out_shape=jax.ShapeDtypeStruct((b, s, d), F32),
        compiler_params=_params(("parallel", "parallel"), "mix_out"),
        name="mix_out",
    )(y_ret, y_sb, p, p, x, mod, w_ret_out, w_sb_out, w_mix_out)


def _mlp_kernel(x_ref, g_ref, mod_ref, wu_ref, wd_ref, fg_ref, o_ref, acc_ref, *, final_norm, tf):
    x = x_ref[0]
    h = _norm_modulate(x, g_ref[...], mod_ref[0, 3:4, :], mod_ref[0, 4:5, :]).astype(BF16)
    for f in range(wu_ref.shape[1] // tf):
        up = jnp.dot(h, wu_ref[:, f * tf:(f + 1) * tf], preferred_element_type=F32)
        u = jnp.square(jnp.maximum(up, 0.0)).astype(BF16)
        part = jnp.dot(u, wd_ref[f * tf:(f + 1) * tf, :], preferred_element_type=F32)
        if f == 0:
            acc_ref[...] = part
        else:
            acc_ref[...] += part
    y = x + mod_ref[0, 5:6, :] * acc_ref[...]
    if final_norm:
        ms = jnp.mean(y * y, axis=-1, keepdims=True)
        y = y * lax.rsqrt(ms + EPS) * fg_ref[...]
    o_ref[0] = y


def _mlp(x, g, mod, w_up, w_down, final_g, final_norm):
    b, s, d = x.shape
    ff = w_up.shape[1]
    tm, tf = MLP_ROWS, MLP_FF_CHUNK
    const = lambda bi, i: (0, 0)
    return pl.pallas_call(
        functools.partial(_mlp_kernel, final_norm=final_norm, tf=tf),
        grid=(b, s // tm),
        in_specs=[
            pl.BlockSpec((1, tm, d), lambda bi, i: (bi, i, 0)),
            pl.BlockSpec((1, d), const),
            pl.BlockSpec((1, N_MOD, d), lambda bi, i: (bi, 0, 0)),
            pl.BlockSpec((d, ff), const, pipeline_mode=pl.Buffered(1)),
            pl.BlockSpec((ff, d), const, pipeline_mode=pl.Buffered(1)),
            pl.BlockSpec((1, d), const),
        ],
        out_specs=pl.BlockSpec((1, tm, d), lambda bi, i: (bi, i, 0)),
        out_shape=jax.ShapeDtypeStruct((b, s, d), F32),
        scratch_shapes=[pltpu.VMEM((tm, d), F32)],
        compiler_params=_params(("parallel", "parallel"), "mlp"),
        name="mlp",
    )(x, g.reshape(1, d), mod, w_up, w_down, final_g.reshape(1, d))


def _rotary_tables(s, dk):
    half = dk // 2
    inv_freq = np.power(ROPE_BASE, -np.arange(half, dtype=np.float64) / half)
    ang = np.arange(s, dtype=np.float64)[:, None] * inv_freq[None, :]
    return jnp.asarray(np.cos(ang), F32), jnp.asarray(np.sin(ang), F32)


def _decay_table():
    log_gamma = np.log1p(-np.power(2.0, -5.0 - np.arange(RET_HEADS, dtype=np.float64)))
    return jnp.asarray(np.stack([log_gamma, np.exp(RET_CHUNK * log_gamma)]), F32)


def _projection_column_scale(d_model, n):
    col = np.arange(n)
    is_sb_q = (col >= 6 * d_model) & (col < 6 * d_model + SB_HEADS * SB_DIM)
    return jnp.asarray(np.where(is_sb_q, LOG2E * SB_DIM ** -0.5, 1.0).reshape(1, n), F32)


def _later_key_ones(t):
    return jnp.asarray(np.tril(np.ones((t, t)), -1), BF16)


def kernel(x, c, norm_mix_g, w_in, w_ret_out, w_sb_out, w_mix_out, norm_mlp_g, w_up, w_down,
           w_ada, b_ada, final_g):
    depth = w_in.shape[0]
    b, s, d = x.shape
    cos, sin = _rotary_tables(s, d // RET_HEADS)
    lg_tab = _decay_table()
    col_scale = _projection_column_scale(d, w_in.shape[2])
    u = _later_key_ones(SB_SUB)

    mod_all = _ada_modulation(c, w_ada, b_ada).reshape(depth, b, N_MOD, d)
    for l in range(depth):
        mod = mod_all[l]
        p = _in_projection(x, norm_mix_g[l], mod, w_in[l].astype(BF16), col_scale)
        y_ret = _retention(p, lg_tab, cos, sin, d)
        y_sb = _stick_breaking(p, u, d)
        x = _mix_out(y_ret, y_sb, p, x, mod, w_ret_out[l].astype(BF16),
                     w_sb_out[l].astype(BF16), w_mix_out[l].astype(BF16))
        x = _mlp(x, norm_mlp_g[l], mod, w_up[l].astype(BF16), w_down[l].astype(BF16),
                 final_g, final_norm=(l == depth - 1))
    return x
```

```python
import functools

import jax
import jax.numpy as jnp
import numpy as np
from jax import lax
from jax.experimental import pallas as pl
from jax.experimental.pallas import tpu as pltpu

F32 = jnp.float32
BF16 = jnp.bfloat16

RET_HEADS = 4
SB_HEADS = 16
SB_DIM = 64
N_MOD = 6
ROPE_BASE = 10000.0
EPS = 1e-6
GN_EPS = 1e-5

RET_CHUNK = 256
SB_SUB = 256
SB_Q_SUBS = 16
SB_EXIT_GROUP = 2
SB_STRAIGHT_WAVES = 3
NEG_BIG = -1e30
SB_SATURATED = 152.0
SB_SOFTPLUS_CLAMP = 64.0
LOG2E = 1.4426950408889634
LANES = 128
MIB = 1024 * 1024

ADA_COL_TILES = 4
IN_PROJ_ROWS = 1024
IN_PROJ_COL_TILES = 4
MIX_OUT_ROWS = 512
MLP_ROWS = 1024
MLP_FF_CHUNK = 1024
VMEM_LIMIT_MIB = {"ada": 32, "in_proj": 56, "retention": 32, "stick_breaking": 32,
                  "mix_out": 40, "mlp": 56}


def _params(semantics, call):
    return pltpu.CompilerParams(dimension_semantics=semantics,
                                vmem_limit_bytes=VMEM_LIMIT_MIB[call] * MIB)


def _ada_kernel(c_ref, w_ref, b_ref, o_ref):
    c = c_ref[...]
    c_act = c * jax.nn.sigmoid(c)
    o_ref[0] = jnp.dot(c_act, w_ref[0], preferred_element_type=F32) + b_ref[0]


def _ada_modulation(c, w_ada, b_ada):
    depth, d, n = w_ada.shape
    b = c.shape[0]
    tn = n // ADA_COL_TILES
    return pl.pallas_call(
        _ada_kernel,
        grid=(depth, n // tn),
        in_specs=[
            pl.BlockSpec((b, d), lambda l, j: (0, 0)),
            pl.BlockSpec((1, d, tn), lambda l, j: (l, 0, j)),
            pl.BlockSpec((1, 1, tn), lambda l, j: (l, 0, j)),
        ],
        out_specs=pl.BlockSpec((1, b, tn), lambda l, j: (l, 0, j)),
        out_shape=jax.ShapeDtypeStruct((depth, b, n), F32),
        compiler_params=_params(("parallel", "parallel"), "ada"),
        name="ada_modulation",
    )(c, w_ada, b_ada.reshape(depth, 1, n))


def _norm_modulate(x, g, shift, scale):
    ms = jnp.mean(x * x, axis=-1, keepdims=True)
    y = x * lax.rsqrt(ms + EPS) * g
    return y * (1.0 + scale) + shift


def _inproj_kernel(x_ref, g_ref, mod_ref, w_ref, cs_ref, o_ref, h_ref):
    @pl.when(pl.program_id(2) == 0)
    def _():
        h = _norm_modulate(x_ref[0], g_ref[...], mod_ref[0, 0:1, :], mod_ref[0, 1:2, :])
        h_ref[...] = h.astype(BF16)

    o_ref[0] = (jnp.dot(h_ref[...], w_ref[...], preferred_element_type=F32)
                * cs_ref[...]).astype(BF16)


def _in_projection(x, g, mod, w_in, col_scale):
    b, s, d = x.shape
    n = w_in.shape[1]
    tm, tn = IN_PROJ_ROWS, n // IN_PROJ_COL_TILES
    return pl.pallas_call(
        _inproj_kernel,
        grid=(b, s // tm, n // tn),
        in_specs=[
            pl.BlockSpec((1, tm, d), lambda bi, i, j: (bi, i, 0)),
            pl.BlockSpec((1, d), lambda bi, i, j: (0, 0)),
            pl.BlockSpec((1, N_MOD, d), lambda bi, i, j: (bi, 0, 0)),
            pl.BlockSpec((d, tn), lambda bi, i, j: (0, j)),
            pl.BlockSpec((1, tn), lambda bi, i, j: (0, j)),
        ],
        out_specs=pl.BlockSpec((1, tm, tn), lambda bi, i, j: (bi, i, j)),
        out_shape=jax.ShapeDtypeStruct((b, s, n), BF16),
        scratch_shapes=[pltpu.VMEM((tm, d), BF16)],
        compiler_params=_params(("parallel", "parallel", "arbitrary"), "in_proj"),
        name="in_projection",
    )(x, g.reshape(1, d), mod, w_in, col_scale)


def _retention_kernel(lg_ref, q_ref, k_ref, v_ref, rg_ref, cos_ref, sin_ref, o_ref,
                      r_ref, dec_ref, xi_ref, zeta_ref):
    c = q_ref.shape[1]
    n_heads = r_ref.shape[0]
    dk, dv = r_ref.shape[1], r_ref.shape[2]
    half = dk // 2

    @pl.when(pl.program_id(1) == 0)
    def _():
        r_ref[...] = jnp.zeros_like(r_ref)
        rel = (lax.broadcasted_iota(jnp.int32, (c, c), 0)
               - lax.broadcasted_iota(jnp.int32, (c, c), 1)).astype(F32)
        idx = lax.broadcasted_iota(jnp.int32, (c, 1), 0).astype(F32)
        for hd in range(n_heads):
            lg = lg_ref[0, hd]
            dec_ref[hd] = jnp.where(rel >= 0, jnp.exp(jnp.maximum(rel, 0.0) * lg), 0.0)
            xi_ref[hd] = jnp.exp((idx + 1.0) * lg)
            zeta_ref[hd] = jnp.exp((c - 1.0 - idx) * lg)

    cos = cos_ref[...]
    sin = sin_ref[...]

    def rot(t):
        t1, t2 = t[:, :half], t[:, half:]
        return jnp.concatenate([t1 * cos - t2 * sin, t1 * sin + t2 * cos], axis=-1)

    for hd in range(n_heads):
        gamma_c = lg_ref[1, hd]
        qr = rot(q_ref[0, :, hd * dk:(hd + 1) * dk].astype(F32)).astype(BF16)
        kr = rot(k_ref[0, :, hd * dk:(hd + 1) * dk].astype(F32)) * (dk ** -0.5)
        v = v_ref[0, :, hd * dv:(hd + 1) * dv]
        scores = lax.dot_general(qr, kr.astype(BF16), (((1,), (1,)), ((), ())),
                                 preferred_element_type=F32) * dec_ref[hd]
        r_old = r_ref[hd]
        o = jnp.dot(scores.astype(BF16), v, preferred_element_type=F32)
        o = o + jnp.dot(qr, r_old.astype(BF16), preferred_element_type=F32) * xi_ref[hd]
        kz = (kr * zeta_ref[hd]).astype(BF16)
        r_ref[hd] = r_old * gamma_c + lax.dot_general(kz, v, (((0,), (0,)), ((), ())),
                                                      preferred_element_type=F32)
        mu = jnp.mean(o, axis=-1, keepdims=True)
        oc = o - mu
        var = jnp.mean(oc * oc, axis=-1, keepdims=True)
        yn = oc * lax.rsqrt(var + GN_EPS)
        rg = rg_ref[0, :, hd * dv:(hd + 1) * dv].astype(F32)
        o_ref[0, :, hd * dv:(hd + 1) * dv] = (rg * jax.nn.sigmoid(rg) * yn).astype(BF16)


def _retention(p, lg_tab, cos, sin, d_model):
    b, s, _ = p.shape
    dk = d_model // RET_HEADS
    dv = 2 * dk
    qk_w, v_w = RET_HEADS * dk, RET_HEADS * dv
    c = RET_CHUNK
    kq, kk, kv, kg = 0, 1, (2 * qk_w) // v_w, (2 * qk_w) // v_w + 1
    return pl.pallas_call(
        _retention_kernel,
        grid=(b, s // c),
        in_specs=[
            pl.BlockSpec(memory_space=pltpu.SMEM),
            pl.BlockSpec((1, c, qk_w), lambda bi, n: (bi, n, kq)),
            pl.BlockSpec((1, c, qk_w), lambda bi, n: (bi, n, kk)),
            pl.BlockSpec((1, c, v_w), lambda bi, n: (bi, n, kv)),
            pl.BlockSpec((1, c, v_w), lambda bi, n: (bi, n, kg)),
            pl.BlockSpec((c, dk // 2), lambda bi, n: (n, 0)),
            pl.BlockSpec((c, dk // 2), lambda bi, n: (n, 0)),
        ],
        out_specs=pl.BlockSpec((1, c, v_w), lambda bi, n: (bi, n, 0)),
        out_shape=jax.ShapeDtypeStruct((b, s, v_w), BF16),
        scratch_shapes=[
            pltpu.VMEM((RET_HEADS, dk, dv), F32),
            pltpu.VMEM((RET_HEADS, c, c), F32),
            pltpu.VMEM((RET_HEADS, c, 1), F32),
            pltpu.VMEM((RET_HEADS, c, 1), F32),
        ],
        compiler_params=_params(("parallel", "arbitrary"), "retention"),
        name="retention",
    )(lg_tab, p, p, p, p, cos, sin)


def _sb_kernel(q_ref, k_ref, v_ref, u_ref, o_ref, qs_ref, acc_ref, carry_ref):
    t = SB_SUB
    n_sub = SB_Q_SUBS
    rows = 2 * t
    i = pl.program_id(2)
    lane = lax.broadcasted_iota(jnp.int32, (1, LANES), 1)
    first = lane < SB_DIM
    for s in range(n_sub):
        q = q_ref[0, s * t:(s + 1) * t, :]
        zero = jnp.zeros_like(q)
        qs_ref[s * rows:s * rows + t, :] = jnp.where(first, q, zero)
        qs_ref[s * rows + t:(s + 1) * rows, :] = jnp.where(first, zero, q)
    acc_ref[...] = jnp.zeros_like(acc_ref)
    carry_ref[...] = jnp.zeros_like(carry_ref)

    def sub(s):
        return slice(s * rows, (s + 1) * rows)

    def unit(s, j, diagonal=False, check_valid=False):
        start = pl.multiple_of(jnp.maximum(j, 0) * t, t)
        k = k_ref[0, pl.ds(start, t), :]
        v = v_ref[0, pl.ds(start, t), :]
        z = lax.dot_general(qs_ref[sub(s), :], k, (((1,), (1,)), ((), ())),
                            preferred_element_type=F32)
        if diagonal:
            causal = (lax.broadcasted_iota(jnp.int32, (rows, t), 1)
                      < (lax.broadcasted_iota(jnp.int32, (rows, t), 0) & (t - 1)))
            z = jnp.where(causal, z, NEG_BIG)
        sp = jnp.maximum(z, jnp.log(1.0 + jnp.exp2(jnp.minimum(z, SB_SOFTPLUS_CLAMP))) * LOG2E)
        s_excl = jnp.dot(sp.astype(BF16), u_ref[...], preferred_element_type=F32)
        carry = carry_ref[sub(s), :]
        spent = carry - jnp.where(j >= 0, 0.0, NEG_BIG) if check_valid else carry
        x = (z - sp) - (s_excl + jnp.concatenate([spent] * (t // LANES), axis=1))
        carry_ref[sub(s), :] = carry + jnp.broadcast_to(
            jnp.sum(sp, axis=-1, keepdims=True), (rows, LANES))
        acc_ref[sub(s), :] += jnp.dot(jnp.exp2(x).astype(BF16), v, preferred_element_type=F32)

    base = n_sub * i
    for w in range(SB_STRAIGHT_WAVES):
        for s in range(n_sub):
            unit(s, base + s - w, diagonal=(w == 0), check_valid=(s < w))

    def sweep_rest(group):
        def wave_needed(w):
            need = jnp.bool_(False)
            for s in group:
                unsaturated = jnp.min(carry_ref[sub(s), :]) < SB_SATURATED
                need = jnp.logical_or(need, jnp.logical_and(base + s - w >= 0, unsaturated))
            return need

        def wave(state):
            w, _ = state
            for s in group:
                unit(s, base + s - w, check_valid=True)
            return w + 1, wave_needed(w + 1)

        lax.while_loop(lambda state: state[1], wave,
                       (jnp.int32(SB_STRAIGHT_WAVES), wave_needed(SB_STRAIGHT_WAVES)))

    for first_sub in range(0, n_sub, SB_EXIT_GROUP):
        sweep_rest(range(first_sub, first_sub + SB_EXIT_GROUP))
    for s in range(n_sub):
        o_ref[0, s * t:(s + 1) * t, :] = jnp.where(
            first, acc_ref[s * rows:s * rows + t, :], acc_ref[s * rows + t:(s + 1) * rows, :]).astype(BF16)


def _stick_breaking(p, u2, d_model):
    b, s, _ = p.shape
    tq = SB_Q_SUBS * SB_SUB
    pairs = SB_HEADS * SB_DIM // LANES
    base = 6 * d_model // LANES
    return pl.pallas_call(
        _sb_kernel,
        grid=(b, pairs, s // tq),
        in_specs=[
            pl.BlockSpec((1, tq, LANES), lambda bi, hp, i: (bi, i, base + hp)),
            pl.BlockSpec((1, s, LANES), lambda bi, hp, i: (bi, 0, base + pairs + hp)),
            pl.BlockSpec((1, s, LANES), lambda bi, hp, i: (bi, 0, base + 2 * pairs + hp)),
            pl.BlockSpec(u2.shape, lambda bi, hp, i: (0, 0)),
        ],
        out_specs=pl.BlockSpec((1, tq, LANES), lambda bi, hp, i: (bi, i, hp)),
        out_shape=jax.ShapeDtypeStruct((b, s, SB_HEADS * SB_DIM), BF16),
        scratch_shapes=[
            pltpu.VMEM((2 * tq, LANES), BF16),
            pltpu.VMEM((2 * tq, LANES), F32),
            pltpu.VMEM((2 * tq, LANES), F32),
        ],
        compiler_params=_params(("parallel", "parallel", "parallel"), "stick_breaking"),
        name="stick_breaking",
    )(p, p, p, u2)


def _mixout_kernel(yr_ref, ys_ref, ga_ref, gb_ref, x_ref, mod_ref, wr_ref, ws_ref, wm_ref, o_ref):
    ya = jnp.dot(yr_ref[0], wr_ref[...], preferred_element_type=F32)
    yb = jnp.dot(ys_ref[0], ws_ref[...], preferred_element_type=F32)
    merged = (jax.nn.sigmoid(ga_ref[0].astype(F32)) * ya
              + jax.nn.sigmoid(gb_ref[0].astype(F32)) * yb)
    out = jnp.dot(merged.astype(BF16), wm_ref[...], preferred_element_type=F32)
    o_ref[0] = x_ref[0] + mod_ref[0, 2:3, :] * out


def _mix_out(y_ret, y_sb, p, x, mod, w_ret_out, w_sb_out, w_mix_out):
    b, s, d = x.shape
    tm = MIX_OUT_ROWS
    ga_blk = (p.shape[2] - 2 * d) // d
    const = lambda bi, i: (0, 0)
    return pl.pallas_call(
        _mixout_kernel,
        grid=(b, s // tm),
        in_specs=[
            pl.BlockSpec((1, tm, y_ret.shape[2]), lambda bi, i: (bi, i, 0)),
            pl.BlockSpec((1, tm, y_sb.shape[2]), lambda bi, i: (bi, i, 0)),
            pl.BlockSpec((1, tm, d), lambda bi, i: (bi, i, ga_blk)),
            pl.BlockSpec((1, tm, d), lambda bi, i: (bi, i, ga_blk + 1)),
            pl.BlockSpec((1, tm, d), lambda bi, i: (bi, i, 0)),
            pl.BlockSpec((1, N_MOD, d), lambda bi, i: (bi, 0, 0)),
            pl.BlockSpec(w_ret_out.shape, const, pipeline_mode=pl.Buffered(1)),
            pl.BlockSpec(w_sb_out.shape, const, pipeline_mode=pl.Buffered(1)),
            pl.BlockSpec(w_mix_out.shape, const, pipeline_mode=pl.Buffered(1)),
        ],
        out_specs=pl.BlockSpec((1, tm, d), lambda bi, i: (bi, i, 0)),
        out_shape=jax.ShapeDtypeStruct((b, s, d), F32),
        compiler_params=_params(("parallel", "parallel"), "mix_out"),
        name="mix_out",
    )(y_ret, y_sb, p, p, x, mod, w_ret_out, w_sb_out, w_mix_out)


def _mlp_kernel(x_ref, g_ref, mod_ref, wu_ref, wd_ref, fg_ref, o_ref, acc_ref, *, final_norm, tf):
    x = x_ref[0]
    h = _norm_modulate(x, g_ref[...], mod_ref[0, 3:4, :], mod_ref[0, 4:5, :]).astype(BF16)
    for f in range(wu_ref.shape[1] // tf):
        up = jnp.dot(h, wu_ref[:, f * tf:(f + 1) * tf], preferred_element_type=F32)
        u = jnp.square(jnp.maximum(up, 0.0)).astype(BF16)
        part = jnp.dot(u, wd_ref[f * tf:(f + 1) * tf, :], preferred_element_type=F32)
        if f == 0:
            acc_ref[...] = part
        else:
            acc_ref[...] += part
    y = x + mod_ref[0, 5:6, :] * acc_ref[...]
    if final_norm:
        ms = jnp.mean(y * y, axis=-1, keepdims=True)
        y = y * lax.rsqrt(ms + EPS) * fg_ref[...]
    o_ref[0] = y


def _mlp(x, g, mod, w_up, w_down, final_g, final_norm):
    b, s, d = x.shape
    ff = w_up.shape[1]
    tm, tf = MLP_ROWS, MLP_FF_CHUNK
    const = lambda bi, i: (0, 0)
    return pl.pallas_call(
        functools.partial(_mlp_kernel, final_norm=final_norm, tf=tf),
        grid=(b, s // tm),
        in_specs=[
            pl.BlockSpec((1, tm, d), lambda bi, i: (bi, i, 0)),
            pl.BlockSpec((1, d), const),
            pl.BlockSpec((1, N_MOD, d), lambda bi, i: (bi, 0, 0)),
            pl.BlockSpec((d, ff), const, pipeline_mode=pl.Buffered(1)),
            pl.BlockSpec((ff, d), const, pipeline_mode=pl.Buffered(1)),
            pl.BlockSpec((1, d), const),
        ],
        out_specs=pl.BlockSpec((1, tm, d), lambda bi, i: (bi, i, 0)),
        out_shape=jax.ShapeDtypeStruct((b, s, d), F32),
        scratch_shapes=[pltpu.VMEM((tm, d), F32)],
        compiler_params=_params(("parallel", "parallel"), "mlp"),
        name="mlp",
    )(x, g.reshape(1, d), mod, w_up, w_down, final_g.reshape(1, d))


def _rotary_tables(s, dk):
    half = dk // 2
    inv_freq = np.power(ROPE_BASE, -np.arange(half, dtype=np.float64) / half)
    ang = np.arange(s, dtype=np.float64)[:, None] * inv_freq[None, :]
    return jnp.asarray(np.cos(ang), F32), jnp.asarray(np.sin(ang), F32)


def _decay_table():
    log_gamma = np.log1p(-np.power(2.0, -5.0 - np.arange(RET_HEADS, dtype=np.float64)))
    return jnp.asarray(np.stack([log_gamma, np.exp(RET_CHUNK * log_gamma)]), F32)


def _projection_column_scale(d_model, n):
    col = np.arange(n)
    is_sb_q = (col >= 6 * d_model) & (col < 6 * d_model + SB_HEADS * SB_DIM)
    return jnp.asarray(np.where(is_sb_q, LOG2E * SB_DIM ** -0.5, 1.0).reshape(1, n), F32)


def _later_key_ones(t):
    return jnp.asarray(np.tril(np.ones((t, t)), -1), BF16)


def kernel(x, c, norm_mix_g, w_in, w_ret_out, w_sb_out, w_mix_out, norm_mlp_g, w_up, w_down,
           w_ada, b_ada, final_g):
    depth = w_in.shape[0]
    b, s, d = x.shape
    cos, sin = _rotary_tables(s, d // RET_HEADS)
    lg_tab = _decay_table()
    col_scale = _projection_column_scale(d, w_in.shape[2])
    u = _later_key_ones(SB_SUB)

    mod_all = _ada_modulation(c, w_ada, b_ada).reshape(depth, b, N_MOD, d)
    for l in range(depth):
        mod = mod_all[l]
        p = _in_projection(x, norm_mix_g[l], mod, w_in[l].astype(BF16), col_scale)
        y_ret = _retention(p, lg_tab, cos, sin, d)
        y_sb = _stick_breaking(p, u, d)
        x = _mix_out(y_ret, y_sb, p, x, mod, w_ret_out[l].astype(BF16),
                     w_sb_out[l].astype(BF16), w_mix_out[l].astype(BF16))
        x = _mlp(x, norm_mlp_g[l], mod, w_up[l].astype(BF16), w_down[l].astype(BF16),
                 final_g, final_norm=(l == depth - 1))
    return x
```

```python
import functools

import jax
import jax.numpy as jnp
import numpy as np
from jax import lax
from jax.experimental import pallas as pl
from jax.experimental.pallas import tpu as pltpu

F32 = jnp.float32
BF16 = jnp.bfloat16

RET_HEADS = 4
SB_HEADS = 16
SB_DIM = 64
N_MOD = 6
ROPE_BASE = 10000.0
EPS = 1e-6
GN_EPS = 1e-5

RET_CHUNK = 256
SB_SUB = 256
SB_Q_SUBS = 16
SB_EXIT_GROUP = 2
SB_STRAIGHT_WAVES = 3
NEG_BIG = -1e30
SB_SATURATED = 152.0
SB_SOFTPLUS_CLAMP = 64.0
LOG2E = 1.4426950408889634
LANES = 128
MIB = 1024 * 1024

ADA_COL_TILES = 4
IN_PROJ_ROWS = 1024
IN_PROJ_COL_TILES = 4
TAIL_ROWS = 512
MLP_FF_CHUNK = 1024
VMEM_LIMIT_MIB = {"ada": 32, "in_proj": 56, "retention": 32, "stick_breaking": 32,
                  "tails": 56}


def _params(semantics, call):
    return pltpu.CompilerParams(dimension_semantics=semantics,
                                vmem_limit_bytes=VMEM_LIMIT_MIB[call] * MIB)


def _ada_kernel(c_ref, w_ref, b_ref, o_ref):
    c = c_ref[...]
    c_act = c * jax.nn.sigmoid(c)
    o_ref[0] = jnp.dot(c_act, w_ref[0], preferred_element_type=F32) + b_ref[0]


def _ada_modulation(c, w_ada, b_ada):
    depth, d, n = w_ada.shape
    b = c.shape[0]
    tn = n // ADA_COL_TILES
    return pl.pallas_call(
        _ada_kernel,
        grid=(depth, n // tn),
        in_specs=[
            pl.BlockSpec((b, d), lambda l, j: (0, 0)),
            pl.BlockSpec((1, d, tn), lambda l, j: (l, 0, j)),
            pl.BlockSpec((1, 1, tn), lambda l, j: (l, 0, j)),
        ],
        out_specs=pl.BlockSpec((1, b, tn), lambda l, j: (l, 0, j)),
        out_shape=jax.ShapeDtypeStruct((depth, b, n), F32),
        compiler_params=_params(("parallel", "parallel"), "ada"),
        name="ada_modulation",
    )(c, w_ada, b_ada.reshape(depth, 1, n))


def _norm_modulate(x, g, shift, scale):
    ms = jnp.mean(x * x, axis=-1, keepdims=True)
    y = x * lax.rsqrt(ms + EPS) * g
    return y * (1.0 + scale) + shift


def _inproj_kernel(x_ref, g_ref, mod_ref, w_ref, cs_ref, o_ref, h_ref):
    @pl.when(pl.program_id(2) == 0)
    def _():
        h = _norm_modulate(x_ref[0], g_ref[...], mod_ref[0, 0:1, :], mod_ref[0, 1:2, :])
        h_ref[...] = h.astype(BF16)

    o_ref[0] = (jnp.dot(h_ref[...], w_ref[...], preferred_element_type=F32)
                * cs_ref[...]).astype(BF16)


def _in_projection(x, g, mod, w_in, col_scale):
    b, s, d = x.shape
    n = w_in.shape[1]
    tm, tn = IN_PROJ_ROWS, n // IN_PROJ_COL_TILES
    return pl.pallas_call(
        _inproj_kernel,
        grid=(b, s // tm, n // tn),
        in_specs=[
            pl.BlockSpec((1, tm, d), lambda bi, i, j: (bi, i, 0)),
            pl.BlockSpec((1, d), lambda bi, i, j: (0, 0)),
            pl.BlockSpec((1, N_MOD, d), lambda bi, i, j: (bi, 0, 0)),
            pl.BlockSpec((d, tn), lambda bi, i, j: (0, j)),
            pl.BlockSpec((1, tn), lambda bi, i, j: (0, j)),
        ],
        out_specs=pl.BlockSpec((1, tm, tn), lambda bi, i, j: (bi, i, j)),
        out_shape=jax.ShapeDtypeStruct((b, s, n), BF16),
        scratch_shapes=[pltpu.VMEM((tm, d), BF16)],
        compiler_params=_params(("parallel", "parallel", "arbitrary"), "in_proj"),
        name="in_projection",
    )(x, g.reshape(1, d), mod, w_in, col_scale)


def _retention_kernel(lg_ref, q_ref, k_ref, v_ref, rg_ref, cos_ref, sin_ref, o_ref,
                      r_ref, dec_ref, xi_ref, zeta_ref):
    c = q_ref.shape[1]
    n_heads = r_ref.shape[0]
    dk, dv = r_ref.shape[1], r_ref.shape[2]
    half = dk // 2

    @pl.when(pl.program_id(1) == 0)
    def _():
        r_ref[...] = jnp.zeros_like(r_ref)
        rel = (lax.broadcasted_iota(jnp.int32, (c, c), 0)
               - lax.broadcasted_iota(jnp.int32, (c, c), 1)).astype(F32)
        idx = lax.broadcasted_iota(jnp.int32, (c, 1), 0).astype(F32)
        for hd in range(n_heads):
            lg = lg_ref[0, hd]
            dec_ref[hd] = jnp.where(rel >= 0, jnp.exp(jnp.maximum(rel, 0.0) * lg), 0.0)
            xi_ref[hd] = jnp.exp((idx + 1.0) * lg)
            zeta_ref[hd] = jnp.exp((c - 1.0 - idx) * lg)

    cos = cos_ref[...]
    sin = sin_ref[...]

    def rot(t):
        t1, t2 = t[:, :half], t[:, half:]
        return jnp.concatenate([t1 * cos - t2 * sin, t1 * sin + t2 * cos], axis=-1)

    for hd in range(n_heads):
        gamma_c = lg_ref[1, hd]
        qr = rot(q_ref[0, :, hd * dk:(hd + 1) * dk].astype(F32)).astype(BF16)
        kr = rot(k_ref[0, :, hd * dk:(hd + 1) * dk].astype(F32)) * (dk ** -0.5)
        v = v_ref[0, :, hd * dv:(hd + 1) * dv]
        scores = lax.dot_general(qr, kr.astype(BF16), (((1,), (1,)), ((), ())),
                                 preferred_element_type=F32) * dec_ref[hd]
        r_old = r_ref[hd]
        o = jnp.dot(scores.astype(BF16), v, preferred_element_type=F32)
        o = o + jnp.dot(qr, r_old.astype(BF16), preferred_element_type=F32) * xi_ref[hd]
        kz = (kr * zeta_ref[hd]).astype(BF16)
        r_ref[hd] = r_old * gamma_c + lax.dot_general(kz, v, (((0,), (0,)), ((), ())),
                                                      preferred_element_type=F32)
        mu = jnp.mean(o, axis=-1, keepdims=True)
        oc = o - mu
        var = jnp.mean(oc * oc, axis=-1, keepdims=True)
        yn = oc * lax.rsqrt(var + GN_EPS)
        rg = rg_ref[0, :, hd * dv:(hd + 1) * dv].astype(F32)
        o_ref[0, :, hd * dv:(hd + 1) * dv] = (rg * jax.nn.sigmoid(rg) * yn).astype(BF16)


def _retention(p, lg_tab, cos, sin, d_model):
    b, s, _ = p.shape
    dk = d_model // RET_HEADS
    dv = 2 * dk
    qk_w, v_w = RET_HEADS * dk, RET_HEADS * dv
    c = RET_CHUNK
    kq, kk, kv, kg = 0, 1, (2 * qk_w) // v_w, (2 * qk_w) // v_w + 1
    return pl.pallas_call(
        _retention_kernel,
        grid=(b, s // c),
        in_specs=[
            pl.BlockSpec(memory_space=pltpu.SMEM),
            pl.BlockSpec((1, c, qk_w), lambda bi, n: (bi, n, kq)),
            pl.BlockSpec((1, c, qk_w), lambda bi, n: (bi, n, kk)),
            pl.BlockSpec((1, c, v_w), lambda bi, n: (bi, n, kv)),
            pl.BlockSpec((1, c, v_w), lambda bi, n: (bi, n, kg)),
            pl.BlockSpec((c, dk // 2), lambda bi, n: (n, 0)),
            pl.BlockSpec((c, dk // 2), lambda bi, n: (n, 0)),
        ],
        out_specs=pl.BlockSpec((1, c, v_w), lambda bi, n: (bi, n, 0)),
        out_shape=jax.ShapeDtypeStruct((b, s, v_w), BF16),
        scratch_shapes=[
            pltpu.VMEM((RET_HEADS, dk, dv), F32),
            pltpu.VMEM((RET_HEADS, c, c), F32),
            pltpu.VMEM((RET_HEADS, c, 1), F32),
            pltpu.VMEM((RET_HEADS, c, 1), F32),
        ],
        compiler_params=_params(("parallel", "arbitrary"), "retention"),
        name="retention",
    )(lg_tab, p, p, p, p, cos, sin)


def _sb_kernel(q_ref, k_ref, v_ref, u_ref, o_ref, qs_ref, acc_ref, carry_ref):
    t = SB_SUB
    n_sub = SB_Q_SUBS
    rows = 2 * t
    i = pl.program_id(2)
    lane = lax.broadcasted_iota(jnp.int32, (1, LANES), 1)
    first = lane < SB_DIM
    for s in range(n_sub):
        q = q_ref[0, s * t:(s + 1) * t, :]
        zero = jnp.zeros_like(q)
        qs_ref[s * rows:s * rows + t, :] = jnp.where(first, q, zero)
        qs_ref[s * rows + t:(s + 1) * rows, :] = jnp.where(first, zero, q)
    acc_ref[...] = jnp.zeros_like(acc_ref)
    carry_ref[...] = jnp.zeros_like(carry_ref)

    def sub(s):
        return slice(s * rows, (s + 1) * rows)

    def unit(s, j, diagonal=False, check_valid=False):
        start = pl.multiple_of(jnp.maximum(j, 0) * t, t)
        k = k_ref[0, pl.ds(start, t), :]
        v = v_ref[0, pl.ds(start, t), :]
        z = lax.dot_general(qs_ref[sub(s), :], k, (((1,), (1,)), ((), ())),
                            preferred_element_type=F32)
        if diagonal:
            causal = (lax.broadcasted_iota(jnp.int32, (rows, t), 1)
                      < (lax.broadcasted_iota(jnp.int32, (rows, t), 0) & (t - 1)))
            z = jnp.where(causal, z, NEG_BIG)
        sp = jnp.maximum(z, jnp.log(1.0 + jnp.exp2(jnp.minimum(z, SB_SOFTPLUS_CLAMP))) * LOG2E)
        s_excl = jnp.dot(sp.astype(BF16), u_ref[...], preferred_element_type=F32)
        carry = carry_ref[sub(s), :]
        spent = carry - jnp.where(j >= 0, 0.0, NEG_BIG) if check_valid else carry
        x = (z - sp) - (s_excl + jnp.concatenate([spent] * (t // LANES), axis=1))
        carry_ref[sub(s), :] = carry + jnp.broadcast_to(
            jnp.sum(sp, axis=-1, keepdims=True), (rows, LANES))
        acc_ref[sub(s), :] += jnp.dot(jnp.exp2(x).astype(BF16), v, preferred_element_type=F32)

    base = n_sub * i
    for w in range(SB_STRAIGHT_WAVES):
        for s in range(n_sub):
            unit(s, base + s - w, diagonal=(w == 0), check_valid=(s < w))

    def sweep_rest(group):
        def wave_needed(w):
            need = jnp.bool_(False)
            for s in group:
                unsaturated = jnp.min(carry_ref[sub(s), :]) < SB_SATURATED
                need = jnp.logical_or(need, jnp.logical_and(base + s - w >= 0, unsaturated))
            return need

        def wave(state):
            w, _ = state
            for s in group:
                unit(s, base + s - w, check_valid=True)
            return w + 1, wave_needed(w + 1)

        lax.while_loop(lambda state: state[1], wave,
                       (jnp.int32(SB_STRAIGHT_WAVES), wave_needed(SB_STRAIGHT_WAVES)))

    for first_sub in range(0, n_sub, SB_EXIT_GROUP):
        sweep_rest(range(first_sub, first_sub + SB_EXIT_GROUP))
    for s in range(n_sub):
        o_ref[0, s * t:(s + 1) * t, :] = jnp.where(
            first, acc_ref[s * rows:s * rows + t, :], acc_ref[s * rows + t:(s + 1) * rows, :]).astype(BF16)


def _stick_breaking(p, u2, d_model):
    b, s, _ = p.shape
    tq = SB_Q_SUBS * SB_SUB
    pairs = SB_HEADS * SB_DIM // LANES
    base = 6 * d_model // LANES
    return pl.pallas_call(
        _sb_kernel,
        grid=(b, pairs, s // tq),
        in_specs=[
            pl.BlockSpec((1, tq, LANES), lambda bi, hp, i: (bi, i, base + hp)),
            pl.BlockSpec((1, s, LANES), lambda bi, hp, i: (bi, 0, base + pairs + hp)),
            pl.BlockSpec((1, s, LANES), lambda bi, hp, i: (bi, 0, base + 2 * pairs + hp)),
            pl.BlockSpec(u2.shape, lambda bi, hp, i: (0, 0)),
        ],
        out_specs=pl.BlockSpec((1, tq, LANES), lambda bi, hp, i: (bi, i, hp)),
        out_shape=jax.ShapeDtypeStruct((b, s, SB_HEADS * SB_DIM), BF16),
        scratch_shapes=[
            pltpu.VMEM((2 * tq, LANES), BF16),
            pltpu.VMEM((2 * tq, LANES), F32),
            pltpu.VMEM((2 * tq, LANES), F32),
        ],
        compiler_params=_params(("parallel", "parallel", "parallel"), "stick_breaking"),
        name="stick_breaking",
    )(p, p, p, u2)


def _tail_kernel(yr_ref, ys_ref, ga_ref, gb_ref, x_ref, mod_ref, g_ref, wr_ref, ws_ref, wm_ref,
                 wu_ref, wd_ref, fg_ref, o_ref, acc_ref, *, final_norm, tf):
    ya = jnp.dot(yr_ref[0], wr_ref[...], preferred_element_type=F32)
    yb = jnp.dot(ys_ref[0], ws_ref[...], preferred_element_type=F32)
    merged = (jax.nn.sigmoid(ga_ref[0].astype(F32)) * ya
              + jax.nn.sigmoid(gb_ref[0].astype(F32)) * yb)
    out = jnp.dot(merged.astype(BF16), wm_ref[...], preferred_element_type=F32)
    x = x_ref[0] + mod_ref[0, 2:3, :] * out
    h = _norm_modulate(x, g_ref[...], mod_ref[0, 3:4, :], mod_ref[0, 4:5, :]).astype(BF16)
    for f in range(wu_ref.shape[1] // tf):
        up = jnp.dot(h, wu_ref[:, f * tf:(f + 1) * tf], preferred_element_type=F32)
        u = jnp.square(jnp.maximum(up, 0.0)).astype(BF16)
        part = jnp.dot(u, wd_ref[f * tf:(f + 1) * tf, :], preferred_element_type=F32)
        if f == 0:
            acc_ref[...] = part
        else:
            acc_ref[...] += part
    y = x + mod_ref[0, 5:6, :] * acc_ref[...]
    if final_norm:
        ms = jnp.mean(y * y, axis=-1, keepdims=True)
        y = y * lax.rsqrt(ms + EPS) * fg_ref[...]
    o_ref[0] = y


def _sublayer_tails(y_ret, y_sb, p, x, mod, g, w_ret_out, w_sb_out, w_mix_out, w_up, w_down,
                    final_g, final_norm):
    b, s, d = x.shape
    tm, tf = TAIL_ROWS, MLP_FF_CHUNK
    ga_blk = (p.shape[2] - 2 * d) // d
    const = lambda bi, i: (0, 0)
    rows = lambda width, blk=0: pl.BlockSpec((1, tm, width), lambda bi, i: (bi, i, blk))
    resident = lambda w: pl.BlockSpec(w.shape, const, pipeline_mode=pl.Buffered(1))
    return pl.pallas_call(
        functools.partial(_tail_kernel, final_norm=final_norm, tf=tf),
        grid=(b, s // tm),
        in_specs=[
            rows(y_ret.shape[2]), rows(y_sb.shape[2]), rows(d, ga_blk), rows(d, ga_blk + 1), rows(d),
            pl.BlockSpec((1, N_MOD, d), lambda bi, i: (bi, 0, 0)),
            pl.BlockSpec((1, d), const),
            resident(w_ret_out), resident(w_sb_out), resident(w_mix_out),
            resident(w_up), resident(w_down),
            pl.BlockSpec((1, d), const),
        ],
        out_specs=rows(d),
        out_shape=jax.ShapeDtypeStruct((b, s, d), F32),
        scratch_shapes=[pltpu.VMEM((tm, d), F32)],
        compiler_params=_params(("parallel", "parallel"), "tails"),
        name="sublayer_tails",
    )(y_ret, y_sb, p, p, x, mod, g.reshape(1, d), w_ret_out, w_sb_out, w_mix_out, w_up, w_down,
      final_g.reshape(1, d))


def _rotary_tables(s, dk):
    half = dk // 2
    inv_freq = np.power(ROPE_BASE, -np.arange(half, dtype=np.float64) / half)
    ang = np.arange(s, dtype=np.float64)[:, None] * inv_freq[None, :]
    return jnp.asarray(np.cos(ang), F32), jnp.asarray(np.sin(ang), F32)


def _decay_table():
    log_gamma = np.log1p(-np.power(2.0, -5.0 - np.arange(RET_HEADS, dtype=np.float64)))
    return jnp.asarray(np.stack([log_gamma, np.exp(RET_CHUNK * log_gamma)]), F32)


def _projection_column_scale(d_model, n):
    col = np.arange(n)
    is_sb_q = (col >= 6 * d_model) & (col < 6 * d_model + SB_HEADS * SB_DIM)
    return jnp.asarray(np.where(is_sb_q, LOG2E * SB_DIM ** -0.5, 1.0).reshape(1, n), F32)


def _later_key_ones(t):
    return jnp.asarray(np.tril(np.ones((t, t)), -1), BF16)


def kernel(x, c, norm_mix_g, w_in, w_ret_out, w_sb_out, w_mix_out, norm_mlp_g, w_up, w_down,
           w_ada, b_ada, final_g):
    depth = w_in.shape[0]
    b, s, d = x.shape
    cos, sin = _rotary_tables(s, d // RET_HEADS)
    lg_tab = _decay_table()
    col_scale = _projection_column_scale(d, w_in.shape[2])
    u = _later_key_ones(SB_SUB)

    mod_all = _ada_modulation(c, w_ada, b_ada).reshape(depth, b, N_MOD, d)
    for l in range(depth):
        mod = mod_all[l]
        p = _in_projection(x, norm_mix_g[l], mod, w_in[l].astype(BF16), col_scale)
        y_ret = _retention(p, lg_tab, cos, sin, d)
        y_sb = _stick_breaking(p, u, d)
        x = _sublayer_tails(y_ret, y_sb, p, x, mod, norm_mlp_g[l], w_ret_out[l].astype(BF16),
                            w_sb_out[l].astype(BF16), w_mix_out[l].astype(BF16),
                            w_up[l].astype(BF16), w_down[l].astype(BF16), final_g,
                            final_norm=(l == depth - 1))
    return x
```

```python
import functools

import jax
import jax.numpy as jnp
import numpy as np
from jax import lax
from jax.experimental import pallas as pl
from jax.experimental.pallas import tpu as pltpu

F32 = jnp.float32
BF16 = jnp.bfloat16

RET_HEADS = 4
SB_HEADS = 16
SB_DIM = 64
N_MOD = 6
ROPE_BASE = 10000.0
EPS = 1e-6
GN_EPS = 1e-5

RET_CHUNK = 256
SB_SUB = 256
SB_Q_SUBS = 16
SB_EXIT_GROUP = 2
SB_STRAIGHT_WAVES = 3
NEG_BIG = -1e30
SB_SATURATED = 152.0
SB_SOFTPLUS_CLAMP = 64.0
LOG2E = 1.4426950408889634
LANES = 128
MIB = 1024 * 1024

ADA_COL_TILES = 4
IN_PROJ_ROWS = 1024
IN_PROJ_COL_TILES = 4
TAIL_ROWS = 512
MLP_FF_CHUNK = 1024
VMEM_LIMIT_MIB = {"ada": 32, "in_proj": 56, "retention": 32, "stick_breaking": 32,
                  "tails": 56}


def _params(semantics, call):
    return pltpu.CompilerParams(dimension_semantics=semantics,
                                vmem_limit_bytes=VMEM_LIMIT_MIB[call] * MIB)


def _ada_kernel(c_ref, w_ref, b_ref, o_ref):
    c = c_ref[...]
    c_act = c * jax.nn.sigmoid(c)
    o_ref[0] = jnp.dot(c_act, w_ref[0], preferred_element_type=F32) + b_ref[0]


def _ada_modulation(c, w_ada, b_ada):
    depth, d, n = w_ada.shape
    b = c.shape[0]
    tn = n // ADA_COL_TILES
    return pl.pallas_call(
        _ada_kernel,
        grid=(depth, n // tn),
        in_specs=[
            pl.BlockSpec((b, d), lambda l, j: (0, 0)),
            pl.BlockSpec((1, d, tn), lambda l, j: (l, 0, j)),
            pl.BlockSpec((1, 1, tn), lambda l, j: (l, 0, j)),
        ],
        out_specs=pl.BlockSpec((1, b, tn), lambda l, j: (l, 0, j)),
        out_shape=jax.ShapeDtypeStruct((depth, b, n), F32),
        compiler_params=_params(("parallel", "parallel"), "ada"),
        name="ada_modulation",
    )(c, w_ada, b_ada.reshape(depth, 1, n))


def _norm_modulate(x, g, shift, scale):
    ms = jnp.mean(x * x, axis=-1, keepdims=True)
    y = x * lax.rsqrt(ms + EPS) * g
    return y * (1.0 + scale) + shift


def _inproj_kernel(x_ref, g_ref, mod_ref, w_ref, cs_ref, o_ref, h_ref):
    @pl.when(pl.program_id(2) == 0)
    def _():
        h = _norm_modulate(x_ref[0], g_ref[...], mod_ref[0, 0:1, :], mod_ref[0, 1:2, :])
        h_ref[...] = h.astype(BF16)

    o_ref[0] = (jnp.dot(h_ref[...], w_ref[...], preferred_element_type=F32)
                * cs_ref[...]).astype(BF16)


def _in_projection(x, g, mod, w_in, col_scale):
    b, s, d = x.shape
    n = w_in.shape[1]
    tm, tn = IN_PROJ_ROWS, n // IN_PROJ_COL_TILES
    return pl.pallas_call(
        _inproj_kernel,
        grid=(b, s // tm, n // tn),
        in_specs=[
            pl.BlockSpec((1, tm, d), lambda bi, i, j: (bi, i, 0)),
            pl.BlockSpec((1, d), lambda bi, i, j: (0, 0)),
            pl.BlockSpec((1, N_MOD, d), lambda bi, i, j: (bi, 0, 0)),
            pl.BlockSpec((d, tn), lambda bi, i, j: (0, j)),
            pl.BlockSpec((1, tn), lambda bi, i, j: (0, j)),
        ],
        out_specs=pl.BlockSpec((1, tm, tn), lambda bi, i, j: (bi, i, j)),
        out_shape=jax.ShapeDtypeStruct((b, s, n), BF16),
        scratch_shapes=[pltpu.VMEM((tm, d), BF16)],
        compiler_params=_params(("parallel", "parallel", "arbitrary"), "in_proj"),
        name="in_projection",
    )(x, g.reshape(1, d), mod, w_in, col_scale)


def _retention_kernel(lg_ref, q_ref, k_ref, v_ref, rg_ref, cos_ref, sin_ref, o_ref,
                      r_ref, dec_ref, xi_ref, zeta_ref):
    c = q_ref.shape[1]
    n_heads = r_ref.shape[0]
    dk, dv = r_ref.shape[1], r_ref.shape[2]
    half = dk // 2

    @pl.when(pl.program_id(1) == 0)
    def _():
        r_ref[...] = jnp.zeros_like(r_ref)
        rel = (lax.broadcasted_iota(jnp.int32, (c, c), 0)
               - lax.broadcasted_iota(jnp.int32, (c, c), 1)).astype(F32)
        idx = lax.broadcasted_iota(jnp.int32, (c, 1), 0).astype(F32)
        for hd in range(n_heads):
            lg = lg_ref[0, hd]
            dec_ref[hd] = jnp.where(rel >= 0, jnp.exp(jnp.maximum(rel, 0.0) * lg), 0.0)
            xi_ref[hd] = jnp.exp((idx + 1.0) * lg)
            zeta_ref[hd] = jnp.exp((c - 1.0 - idx) * lg)

    cos = cos_ref[...]
    sin = sin_ref[...]

    def rot(t):
        t1, t2 = t[:, :half], t[:, half:]
        return jnp.concatenate([t1 * cos - t2 * sin, t1 * sin + t2 * cos], axis=-1)

    for hd in range(n_heads):
        gamma_c = lg_ref[1, hd]
        qr = rot(q_ref[0, :, hd * dk:(hd + 1) * dk].astype(F32)).astype(BF16)
        kr = rot(k_ref[0, :, hd * dk:(hd + 1) * dk].astype(F32)) * (dk ** -0.5)
        v = v_ref[0, :, hd * dv:(hd + 1) * dv]
        scores = lax.dot_general(qr, kr.astype(BF16), (((1,), (1,)), ((), ())),
                                 preferred_element_type=F32) * dec_ref[hd]
        r_old = r_ref[hd]
        o = jnp.dot(scores.astype(BF16), v, preferred_element_type=F32)
        o = o + jnp.dot(qr, r_old.astype(BF16), preferred_element_type=F32) * xi_ref[hd]
        kz = (kr * zeta_ref[hd]).astype(BF16)
        r_ref[hd] = r_old * gamma_c + lax.dot_general(kz, v, (((0,), (0,)), ((), ())),
                                                      preferred_element_type=F32)
        mu = jnp.mean(o, axis=-1, keepdims=True)
        oc = o - mu
        var = jnp.mean(oc * oc, axis=-1, keepdims=True)
        yn = oc * lax.rsqrt(var + GN_EPS)
        rg = rg_ref[0, :, hd * dv:(hd + 1) * dv].astype(F32)
        o_ref[0, :, hd * dv:(hd + 1) * dv] = (rg * jax.nn.sigmoid(rg) * yn).astype(BF16)


def _retention(p, lg_tab, cos, sin, d_model):
    b, s, _ = p.shape
    dk = d_model // RET_HEADS
    dv = 2 * dk
    qk_w, v_w = RET_HEADS * dk, RET_HEADS * dv
    c = RET_CHUNK
    kq, kk, kv, kg = 0, 1, (2 * qk_w) // v_w, (2 * qk_w) // v_w + 1
    return pl.pallas_call(
        _retention_kernel,
        grid=(b, s // c),
        in_specs=[
            pl.BlockSpec(memory_space=pltpu.SMEM),
            pl.BlockSpec((1, c, qk_w), lambda bi, n: (bi, n, kq)),
            pl.BlockSpec((1, c, qk_w), lambda bi, n: (bi, n, kk)),
            pl.BlockSpec((1, c, v_w), lambda bi, n: (bi, n, kv)),
            pl.BlockSpec((1, c, v_w), lambda bi, n: (bi, n, kg)),
            pl.BlockSpec((c, dk // 2), lambda bi, n: (n, 0)),
            pl.BlockSpec((c, dk // 2), lambda bi, n: (n, 0)),
        ],
        out_specs=pl.BlockSpec((1, c, v_w), lambda bi, n: (bi, n, 0)),
        out_shape=jax.ShapeDtypeStruct((b, s, v_w), BF16),
        scratch_shapes=[
            pltpu.VMEM((RET_HEADS, dk, dv), F32),
            pltpu.VMEM((RET_HEADS, c, c), F32),
            pltpu.VMEM((RET_HEADS, c, 1), F32),
            pltpu.VMEM((RET_HEADS, c, 1), F32),
        ],
        compiler_params=_params(("parallel", "arbitrary"), "retention"),
        name="retention",
    )(lg_tab, p, p, p, p, cos, sin)


def _sb_kernel(q_ref, k_ref, v_ref, u_ref, o_ref, qs_ref, acc_ref, carry_ref):
    t = SB_SUB
    n_sub = SB_Q_SUBS
    rows = 2 * t
    i = pl.program_id(2)
    lane = lax.broadcasted_iota(jnp.int32, (1, LANES), 1)
    first = lane < SB_DIM
    for s in range(n_sub):
        q = q_ref[0, s * t:(s + 1) * t, :]
        zero = jnp.zeros_like(q)
        qs_ref[s * rows:s * rows + t, :] = jnp.where(first, q, zero)
        qs_ref[s * rows + t:(s + 1) * rows, :] = jnp.where(first, zero, q)
    acc_ref[...] = jnp.zeros_like(acc_ref)
    carry_ref[...] = jnp.zeros_like(carry_ref)

    def sub(s):
        return slice(s * rows, (s + 1) * rows)

    def unit(s, j, diagonal=False, check_valid=False):
        start = pl.multiple_of(jnp.maximum(j, 0) * t, t)
        k = k_ref[0, pl.ds(start, t), :]
        v = v_ref[0, pl.ds(start, t), :]
        z = lax.dot_general(qs_ref[sub(s), :], k, (((1,), (1,)), ((), ())),
                            preferred_element_type=F32)
        if diagonal:
            causal = (lax.broadcasted_iota(jnp.int32, (rows, t), 1)
                      < (lax.broadcasted_iota(jnp.int32, (rows, t), 0) & (t - 1)))
            z = jnp.where(causal, z, NEG_BIG)
        sp = jnp.maximum(z, jnp.log(1.0 + jnp.exp2(jnp.minimum(z, SB_SOFTPLUS_CLAMP))) * LOG2E)
        s_excl = jnp.dot(sp.astype(BF16), u_ref[...], preferred_element_type=F32)
        carry = carry_ref[sub(s), :]
        spent = carry - jnp.where(j >= 0, 0.0, NEG_BIG) if check_valid else carry
        x = (z - sp) - (s_excl + jnp.concatenate([spent] * (t // LANES), axis=1))
        carry_ref[sub(s), :] = carry + jnp.broadcast_to(
            jnp.sum(sp, axis=-1, keepdims=True), (rows, LANES))
        acc_ref[sub(s), :] += jnp.dot(jnp.exp2(x).astype(BF16), v, preferred_element_type=F32)

    base = n_sub * i
    for w in range(SB_STRAIGHT_WAVES):
        for s in range(n_sub):
            unit(s, base + s - w, diagonal=(w == 0), check_valid=(s < w))

    def sweep_rest(group):
        def wave_needed(w):
            need = jnp.bool_(False)
            for s in group:
                unsaturated = jnp.min(carry_ref[sub(s), :]) < SB_SATURATED
                need = jnp.logical_or(need, jnp.logical_and(base + s - w >= 0, unsaturated))
            return need

        def wave(state):
            w, _ = state
            for s in group:
                unit(s, base + s - w, check_valid=True)
            return w + 1, wave_needed(w + 1)

        lax.while_loop(lambda state: state[1], wave,
                       (jnp.int32(SB_STRAIGHT_WAVES), wave_needed(SB_STRAIGHT_WAVES)))

    for first_sub in range(0, n_sub, SB_EXIT_GROUP):
        sweep_rest(range(first_sub, first_sub + SB_EXIT_GROUP))
    for s in range(n_sub):
        o_ref[0, s * t:(s + 1) * t, :] = jnp.where(
            first, acc_ref[s * rows:s * rows + t, :], acc_ref[s * rows + t:(s + 1) * rows, :]).astype(BF16)


def _stick_breaking(p, u2, d_model):
    b, s, _ = p.shape
    tq = SB_Q_SUBS * SB_SUB
    pairs = SB_HEADS * SB_DIM // LANES
    base = 6 * d_model // LANES
    return pl.pallas_call(
        _sb_kernel,
        grid=(b, pairs, s // tq),
        in_specs=[
            pl.BlockSpec((1, tq, LANES), lambda bi, hp, i: (bi, i, base + hp)),
            pl.BlockSpec((1, s, LANES), lambda bi, hp, i: (bi, 0, base + pairs + hp)),
            pl.BlockSpec((1, s, LANES), lambda bi, hp, i: (bi, 0, base + 2 * pairs + hp)),
            pl.BlockSpec(u2.shape, lambda bi, hp, i: (0, 0)),
        ],
        out_specs=pl.BlockSpec((1, tq, LANES), lambda bi, hp, i: (bi, i, hp)),
        out_shape=jax.ShapeDtypeStruct((b, s, SB_HEADS * SB_DIM), BF16),
        scratch_shapes=[
            pltpu.VMEM((2 * tq, LANES), BF16),
            pltpu.VMEM((2 * tq, LANES), F32),
            pltpu.VMEM((2 * tq, LANES), F32),
        ],
        compiler_params=_params(("parallel", "parallel", "parallel"), "stick_breaking"),
        name="stick_breaking",
    )(p, p, p, u2)


def _tail_kernel(yr_ref, ys_ref, ga_ref, gb_ref, x_ref, mod_ref, g_ref, wr_ref, ws_ref, wm_ref,
                 wu_ref, wd_ref, fg_ref, o_ref, acc_ref, *, final_norm, tf):
    ya = jnp.dot(yr_ref[0], wr_ref[...], preferred_element_type=F32)
    yb = jnp.dot(ys_ref[0], ws_ref[...], preferred_element_type=F32)
    merged = (jax.nn.sigmoid(ga_ref[0].astype(F32)) * ya
              + jax.nn.sigmoid(gb_ref[0].astype(F32)) * yb)
    out = jnp.dot(merged.astype(BF16), wm_ref[...], preferred_element_type=F32)
    x = x_ref[0] + mod_ref[0, 2:3, :] * out
    h = _norm_modulate(x, g_ref[...], mod_ref[0, 3:4, :], mod_ref[0, 4:5, :]).astype(BF16)
    for f in range(wu_ref.shape[1] // tf):
        up = jnp.dot(h, wu_ref[:, f * tf:(f + 1) * tf], preferred_element_type=F32)
        u = jnp.square(jnp.maximum(up, 0.0)).astype(BF16)
        part = jnp.dot(u, wd_ref[f * tf:(f + 1) * tf, :], preferred_element_type=F32)
        if f == 0:
            acc_ref[...] = part
        else:
            acc_ref[...] += part
    y = x + mod_ref[0, 5:6, :] * acc_ref[...]
    if final_norm:
        ms = jnp.mean(y * y, axis=-1, keepdims=True)
        y = y * lax.rsqrt(ms + EPS) * fg_ref[...]
    o_ref[0] = y


def _sublayer_tails(y_ret, y_sb, p, x, mod, g, w_ret_out, w_sb_out, w_mix_out, w_up, w_down,
                    final_g, final_norm):
    b, s, d = x.shape
    tm, tf = TAIL_ROWS, MLP_FF_CHUNK
    ga_blk = (p.shape[2] - 2 * d) // d
    const = lambda bi, i: (0, 0)
    rows = lambda width, blk=0: pl.BlockSpec((1, tm, width), lambda bi, i: (bi, i, blk))
    resident = lambda w: pl.BlockSpec(w.shape, const, pipeline_mode=pl.Buffered(1))
    return pl.pallas_call(
        functools.partial(_tail_kernel, final_norm=final_norm, tf=tf),
        grid=(b, s // tm),
        in_specs=[
            rows(y_ret.shape[2]), rows(y_sb.shape[2]), rows(d, ga_blk), rows(d, ga_blk + 1), rows(d),
            pl.BlockSpec((1, N_MOD, d), lambda bi, i: (bi, 0, 0)),
            pl.BlockSpec((1, d), const),
            resident(w_ret_out), resident(w_sb_out), resident(w_mix_out),
            resident(w_up), resident(w_down),
            pl.BlockSpec((1, d), const),
        ],
        out_specs=rows(d),
        out_shape=jax.ShapeDtypeStruct((b, s, d), F32),
        scratch_shapes=[pltpu.VMEM((tm, d), F32)],
        compiler_params=pltpu.CompilerParams(
            dimension_semantics=("parallel", "parallel"),
            vmem_limit_bytes=VMEM_LIMIT_MIB["tails"] * MIB,
            allow_input_fusion=[False] * 7 + [True] * 5 + [False]),
        name="sublayer_tails",
    )(y_ret, y_sb, p, p, x, mod, g.reshape(1, d), w_ret_out, w_sb_out, w_mix_out, w_up, w_down,
      final_g.reshape(1, d))


def _rotary_tables(s, dk):
    half = dk // 2
    inv_freq = np.power(ROPE_BASE, -np.arange(half, dtype=np.float64) / half)
    ang = np.arange(s, dtype=np.float64)[:, None] * inv_freq[None, :]
    return jnp.asarray(np.cos(ang), F32), jnp.asarray(np.sin(ang), F32)


def _decay_table():
    log_gamma = np.log1p(-np.power(2.0, -5.0 - np.arange(RET_HEADS, dtype=np.float64)))
    return jnp.asarray(np.stack([log_gamma, np.exp(RET_CHUNK * log_gamma)]), F32)


def _projection_column_scale(d_model, n):
    col = np.arange(n)
    is_sb_q = (col >= 6 * d_model) & (col < 6 * d_model + SB_HEADS * SB_DIM)
    return jnp.asarray(np.where(is_sb_q, LOG2E * SB_DIM ** -0.5, 1.0).reshape(1, n), F32)


def _later_key_ones(t):
    return jnp.asarray(np.tril(np.ones((t, t)), -1), BF16)


def kernel(x, c, norm_mix_g, w_in, w_ret_out, w_sb_out, w_mix_out, norm_mlp_g, w_up, w_down,
           w_ada, b_ada, final_g):
    depth = w_in.shape[0]
    b, s, d = x.shape
    cos, sin = _rotary_tables(s, d // RET_HEADS)
    lg_tab = _decay_table()
    col_scale = _projection_column_scale(d, w_in.shape[2])
    u = _later_key_ones(SB_SUB)

    mod_all = _ada_modulation(c, w_ada, b_ada).reshape(depth, b, N_MOD, d)
    for l in range(depth):
        mod = mod_all[l]
        p = _in_projection(x, norm_mix_g[l], mod, w_in[l].astype(BF16), col_scale)
        y_ret = _retention(p, lg_tab, cos, sin, d)
        y_sb = _stick_breaking(p, u, d)
        x = _sublayer_tails(y_ret, y_sb, p, x, mod, norm_mlp_g[l], w_ret_out[l].astype(BF16),
                            w_sb_out[l].astype(BF16), w_mix_out[l].astype(BF16),
                            w_up[l].astype(BF16), w_down[l].astype(BF16), final_g,
                            final_norm=(l == depth - 1))
    return x
```
